```python
import jax, jax.numpy as jnp
from jax import lax
import numpy as np

D_MODEL = 1024
BATCH = 8
SEQ = 4096
DEPTH = 1

N_META = 16
CHUNK = 64
N_PAD = CHUNK - N_META
GLA_HEADS = 4
GLA_DK = 128
GLA_DV = 256
GLA_KEY = GLA_HEADS * GLA_DK
GLA_VAL = GLA_HEADS * GLA_DV
GATE_RANK = 16
GATE_TAU = 16.0
SSM_WIDTH = D_MODEL
SSM_GROUP = 16
SSM_GROUPS = SSM_WIDTH // SSM_GROUP
SSM_STATE = 64
D_FF = 4 * D_MODEL
EPS = 1e-6

Q_END = GLA_KEY
K_END = Q_END + GLA_KEY
V_END = K_END + GLA_VAL
R_END = V_END + GLA_VAL
A_END = R_END + GATE_RANK
U_END = A_END + SSM_WIDTH
G1_END = U_END + D_MODEL
IN_WIDTH = G1_END + D_MODEL

kernel_name = "hybrid_gla_s5_gated_block"


def rmsnorm(x, g):
    xf = x.astype(jnp.float32)
    ms = jnp.mean(xf * xf, axis=-1, keepdims=True)
    return (xf * lax.rsqrt(ms + EPS) * g.astype(jnp.float32)).astype(x.dtype)


def gla_chunked(q, k, v, log_a):
    bsz, p, h, dk = q.shape
    dv = v.shape[-1]
    nc = p // CHUNK

    def to_chunks(t):
        return t.reshape(bsz, nc, CHUNK, h, t.shape[-1]).transpose(1, 0, 3, 2, 4)

    qc, kc, vc = to_chunks(q), to_chunks(k), to_chunks(v)
    b = jnp.cumsum(to_chunks(log_a).astype(jnp.float32), axis=3)
    b_ref = b[:, :, :, CHUNK // 2:CHUNK // 2 + 1, :]
    q_in = qc * jnp.exp(b - b_ref)
    k_in = kc * jnp.exp(b_ref - b)
    mask = jnp.tril(jnp.ones((CHUNK, CHUNK), dtype=bool))
    scores = jnp.where(mask, jnp.einsum('nbhid,nbhjd->nbhij', q_in, k_in), 0.0)
    o_intra = jnp.einsum('nbhij,nbhjv->nbhiv', scores, vc.astype(jnp.float32))

    b_last = b[:, :, :, -1:, :]
    q_inter = qc * jnp.exp(b)
    k_state = kc * jnp.exp(b_last - b)
    decay_chunk = jnp.exp(b_last[:, :, :, 0, :])

    def step(state, inp):
        qi, ks, vv, dc = inp
        o = jnp.einsum('bhid,bhdv->bhiv', qi, state)
        new_state = dc[..., None] * state + jnp.einsum('bhjd,bhjv->bhdv', ks, vv.astype(jnp.float32))
        return new_state, o

    s0 = jnp.zeros((bsz, h, dk, dv), jnp.float32)
    _, o_inter = lax.scan(step, s0, (q_inter, k_state, vc, decay_chunk))
    o = (o_intra + o_inter).transpose(1, 0, 3, 2, 4).reshape(bsz, p, h, dv)
    return o.astype(v.dtype)


def _complex_combine(e1, e2):
    a1r, a1i, b1r, b1i = e1
    a2r, a2i, b2r, b2i = e2
    return (a2r * a1r - a2i * a1i,
            a2r * a1i + a2i * a1r,
            a2r * b1r - a2i * b1i + b2r,
            a2r * b1i + a2i * b1r + b2i)


def s5_ssm(u, a_re, a_im, log_step, b_re, b_im, c_re, c_im, d_skip):
    bsz, p, _ = u.shape
    g, n, hg = SSM_GROUPS, SSM_STATE, SSM_GROUP
    ar = a_re.astype(jnp.float32)
    ai = a_im.astype(jnp.float32)
    dt = jnp.exp(log_step.astype(jnp.float32))[:, None]
    mag = jnp.exp(ar * dt)
    lam_re, lam_im = mag * jnp.cos(ai * dt), mag * jnp.sin(ai * dt)
    zr, zi = lam_re - 1.0, lam_im
    den = ar * ar + ai * ai
    fr = (zr * ar + zi * ai) / den
    fi = (zi * ar - zr * ai) / den
    br, bi = b_re.astype(jnp.float32), b_im.astype(jnp.float32)
    bb_re = fr[..., None] * br - fi[..., None] * bi
    bb_im = fr[..., None] * bi + fi[..., None] * br
    cr, ci = c_re.astype(jnp.float32), c_im.astype(jnp.float32)

    nc = p // CHUNK
    uc = u.reshape(bsz, nc, CHUNK, g, hg).transpose(1, 0, 2, 3, 4)
    lam_b_re = jnp.broadcast_to(lam_re, (bsz, CHUNK, g, n))
    lam_b_im = jnp.broadcast_to(lam_im, (bsz, CHUNK, g, n))

    def step(carry, u_blk):
        h_re, h_im = carry
        uf = u_blk.astype(jnp.float32)
        bu_re = jnp.einsum('bcgh,gnh->bcgn', uf, bb_re)
        bu_im = jnp.einsum('bcgh,gnh->bcgn', uf, bb_im)
        acc_re, acc_im, x_re, x_im = lax.associative_scan(
            _complex_combine, (lam_b_re, lam_b_im, bu_re, bu_im), axis=1)
        s_re = x_re + acc_re * h_re[:, None] - acc_im * h_im[:, None]
        s_im = x_im + acc_re * h_im[:, None] + acc_im * h_re[:, None]
        y = jnp.einsum('bcgn,ghn->bcgh', s_re, cr) - jnp.einsum('bcgn,ghn->bcgh', s_im, ci)
        return (s_re[:, -1], s_im[:, -1]), y

    carry0 = (jnp.zeros((bsz, g, n), jnp.float32), jnp.zeros((bsz, g, n), jnp.float32))
    _, y = lax.scan(step, carry0, uc)
    y = y.transpose(1, 0, 2, 3, 4).reshape(bsz, p, g * hg)
    return (y + d_skip.astype(jnp.float32) * u.astype(jnp.float32)).astype(u.dtype)


def setup_inputs(seed: int = 0) -> dict:
    key = jax.random.key(seed)
    ks = jax.random.split(key, 24)
    nrm = lambda k, shape, s: jax.random.normal(k, shape, jnp.float32) * s
    gain = lambda k, shape: 1.0 + 0.05 * jax.random.normal(k, shape, jnp.float32)
    L = DEPTH
    x = jax.random.normal(ks[0], (BATCH, SEQ, D_MODEL), jnp.float32)
    meta_tokens = nrm(ks[1], (N_META, D_MODEL), 1.0)
    g_mix_pre = gain(ks[2], (L, D_MODEL))
    w_in = nrm(ks[3], (L, D_MODEL, IN_WIDTH), D_MODEL ** -0.5)
    w_gate_up = nrm(ks[4], (L, GATE_RANK, GLA_KEY), GATE_RANK ** -0.5)
    b_gate = 1.0 + 0.5 * jax.random.normal(ks[5], (L, GLA_KEY), jnp.float32)
    gla_norm_g = gain(ks[6], (L, GLA_HEADS, GLA_DV))
    w_o_gla = nrm(ks[7], (L, GLA_VAL, D_MODEL), GLA_VAL ** -0.5)
    n_idx = jnp.arange(SSM_STATE, dtype=jnp.float32)
    a_re = -0.5 + 0.01 * jax.random.normal(ks[8], (L, SSM_GROUPS, SSM_STATE), jnp.float32)
    a_im = jnp.broadcast_to(jnp.pi * n_idx, (L, SSM_GROUPS, SSM_STATE)) + 0.0
    log_step = jax.random.uniform(ks[9], (L, SSM_GROUPS), jnp.float32,
                                  minval=float(np.log(1e-3)), maxval=float(np.log(1e-1)))
    b_re = nrm(ks[10], (L, SSM_GROUPS, SSM_STATE, SSM_GROUP), (2 * SSM_GROUP) ** -0.5)
    b_im = nrm(ks[11], (L, SSM_GROUPS, SSM_STATE, SSM_GROUP), (2 * SSM_GROUP) ** -0.5)
    c_re = nrm(ks[12], (L, SSM_GROUPS, SSM_GROUP, SSM_STATE), (2 * SSM_STATE) ** -0.5)
    c_im = nrm(ks[13], (L, SSM_GROUPS, SSM_GROUP, SSM_STATE), (2 * SSM_STATE) ** -0.5)
    d_skip = nrm(ks[14], (L, SSM_WIDTH), 1.0)
    w_glu = nrm(ks[15], (L, SSM_WIDTH, 2 * D_MODEL), SSM_WIDTH ** -0.5)
    b_glu = nrm(ks[16], (L, 2 * D_MODEL), 0.02)
    w_out = nrm(ks[17], (L, D_MODEL, D_MODEL), D_MODEL ** -0.5)
    g_mix_post = gain(ks[18], (L, D_MODEL))
    g_ffn_pre = gain(ks[19], (L, D_MODEL))
    w_ff1 = nrm(ks[20], (L, D_MODEL, D_FF), D_MODEL ** -0.5)
    w_ff2 = nrm(ks[21], (L, D_FF, D_MODEL), D_FF ** -0.5)
    g_ffn_post = gain(ks[22], (L, D_MODEL))
    return {"x": x, "meta_tokens": meta_tokens, "g_mix_pre": g_mix_pre, "w_in": w_in,
            "w_gate_up": w_gate_up, "b_gate": b_gate, "gla_norm_g": gla_norm_g, "w_o_gla": w_o_gla,
            "a_re": a_re, "a_im": a_im, "log_step": log_step, "b_re": b_re, "b_im": b_im,
            "c_re": c_re, "c_im": c_im, "d_skip": d_skip, "w_glu": w_glu, "b_glu": b_glu,
            "w_out": w_out, "g_mix_post": g_mix_post, "g_ffn_pre": g_ffn_pre, "w_ff1": w_ff1,
            "w_ff2": w_ff2, "g_ffn_post": g_ffn_post}


def reference(x, meta_tokens, g_mix_pre, w_in, w_gate_up, b_gate, gla_norm_g, w_o_gla,
              a_re, a_im, log_step, b_re, b_im, c_re, c_im, d_skip, w_glu, b_glu,
              w_out, g_mix_post, g_ffn_pre, w_ff1, w_ff2, g_ffn_post):
    bsz, seq, dm = x.shape
    meta = jnp.broadcast_to(meta_tokens.astype(x.dtype)[None], (bsz, N_META, dm))
    h = jnp.concatenate([meta, x], axis=1)
    p = N_PAD + N_META + seq

    def pad(t):
        return jnp.pad(t, ((0, 0), (N_PAD, 0), (0, 0)))

    def heads(t, d):
        return t.reshape(bsz, p, GLA_HEADS, d)

    for l in range(DEPTH):
        xn = rmsnorm(h, g_mix_pre[l])
        proj = xn @ w_in[l]
        q = proj[..., :Q_END]
        k = proj[..., Q_END:K_END]
        v = proj[..., K_END:V_END]
        r = proj[..., V_END:R_END]
        a_low = proj[..., R_END:A_END]
        u = proj[..., A_END:U_END]
        z_gla = proj[..., U_END:G1_END]
        z_ssm = proj[..., G1_END:]

        log_a = jax.nn.log_sigmoid((a_low @ w_gate_up[l] + b_gate[l]).astype(jnp.float32)) / GATE_TAU
        o = gla_chunked(heads(pad(q), GLA_DK) * (GLA_DK ** -0.5), heads(pad(k), GLA_DK),
                        heads(pad(v), GLA_DV), heads(pad(log_a), GLA_DK))[:, N_PAD:]
        o = rmsnorm(o, gla_norm_g[l]).reshape(bsz, N_META + seq, GLA_VAL) * jax.nn.silu(r)
        y_gla = o @ w_o_gla[l]

        y_s = s5_ssm(pad(u), a_re[l], a_im[l], log_step[l], b_re[l], b_im[l],
                     c_re[l], c_im[l], d_skip[l])[:, N_PAD:]
        g_s = jax.nn.gelu(y_s)
        glu = g_s @ w_glu[l] + b_glu[l]
        y_ssm = glu[..., :D_MODEL] * jax.nn.sigmoid(glu[..., D_MODEL:])

        mixed = jax.nn.sigmoid(z_gla) * y_gla + jax.nn.sigmoid(z_ssm) * y_ssm
        h = h + rmsnorm(mixed @ w_out[l], g_mix_post[l])

        hn = rmsnorm(h, g_ffn_pre[l])
        f = jnp.square(jax.nn.relu(hn @ w_ff1[l])) @ w_ff2[l]
        h = h + rmsnorm(f, g_ffn_post[l])

    return h[:, N_META:]
```

```python
import functools

import jax
import jax.numpy as jnp
from jax import lax
from jax.experimental import pallas as pl
from jax.experimental.pallas import tpu as pltpu

F32 = jnp.float32
BF16 = jnp.bfloat16

D_MODEL = 1024
N_META = 16
CHUNK = 64
N_PAD = CHUNK - N_META
HEADS = 4
DK = 128
DV = 256
KEY = HEADS * DK
VAL = HEADS * DV
RANK = 16
TAU = 16.0
GROUPS = 64
GSIZE = 16
NSTATE = 64
WIDTH = GROUPS * GSIZE
D_FF = 4 * D_MODEL
EPS = 1e-6

LANE = 128
SUB = 8
SLAB = 256
SLAB_GROUPS = SLAB // GSIZE
N_SLAB = WIDTH // SLAB
SLAB_STATE = SLAB_GROUPS * NSTATE
VMEM_LIMIT = 56 * 1024 * 1024


def _const_spec(shape):
    nd = len(shape)
    return pl.BlockSpec(shape, lambda *_: (0,) * nd, pipeline_mode=pl.Buffered(1))


def _rms(x, g):
    ms = jnp.mean(x * x, axis=-1, keepdims=True)
    return x * lax.rsqrt(ms + EPS) * g


def _sigmoid(x):
    return 1.0 / (1.0 + jnp.exp(-x))


def _dot(a, b):
    return jnp.dot(a, b, preferred_element_type=F32)


def _inproj_kernel(x_ref, g_ref, wq, wk, wv, wr, wa, wu, wzg, wzs, wup, bg,
                   q_o, k_o, v_o, r_o, la_o, u_o, zg_o, zs_o):
    xn = _rms(x_ref[...], g_ref[...]).astype(BF16)

    def mm(w):
        return _dot(xn, w[...])

    q_o[...] = (mm(wq) * (DK ** -0.5)).astype(BF16)
    k_o[...] = mm(wk).astype(BF16)
    v_o[...] = mm(wv).astype(BF16)
    r = mm(wr)
    r_o[...] = (r * _sigmoid(r)).astype(BF16)
    z = _dot(mm(wa).astype(BF16), wup[...]) + bg[...]
    la_o[...] = (jnp.minimum(z, 0.0) - jnp.log(1.0 + jnp.exp(-jnp.abs(z)))) * (1.0 / TAU)
    u_o[...] = mm(wu).astype(BF16)
    zg_o[...] = _sigmoid(mm(wzg)).astype(BF16)
    zs_o[...] = _sigmoid(mm(wzs)).astype(BF16)


def _inproj(x2, g, ws, wup, bg, *, nb, rows_per_b, tm, u_time_major):
    rows = nb * rows_per_b
    nt = rows_per_b // tm
    row_spec = lambda w: pl.BlockSpec((tm, w), lambda b, i: (b * nt + i, 0))
    if u_time_major:
        u_shape = jax.ShapeDtypeStruct((rows_per_b, nb * WIDTH), BF16)
        u_spec = pl.BlockSpec((tm, WIDTH), lambda b, i: (i, b))
    else:
        u_shape = jax.ShapeDtypeStruct((rows, WIDTH), BF16)
        u_spec = row_spec(WIDTH)
    out_shape = [
        jax.ShapeDtypeStruct((rows, KEY), BF16), jax.ShapeDtypeStruct((rows, KEY), BF16),
        jax.ShapeDtypeStruct((rows, VAL), BF16), jax.ShapeDtypeStruct((rows, VAL), BF16),
        jax.ShapeDtypeStruct((rows, KEY), F32), u_shape,
        jax.ShapeDtypeStruct((rows, D_MODEL), BF16), jax.ShapeDtypeStruct((rows, D_MODEL), BF16),
    ]
    out_specs = [row_spec(KEY), row_spec(KEY), row_spec(VAL), row_spec(VAL), row_spec(KEY),
                 u_spec, row_spec(D_MODEL), row_spec(D_MODEL)]
    in_specs = [row_spec(D_MODEL), _const_spec(g.shape)]
    in_specs += [_const_spec(w.shape) for w in ws]
    in_specs += [_const_spec(wup.shape), _const_spec(bg.shape)]
    return pl.pallas_call(
        _inproj_kernel, grid=(nb, nt), in_specs=in_specs, out_specs=out_specs,
        out_shape=out_shape, name="inproj",
        compiler_params=pltpu.CompilerParams(
            dimension_semantics=("parallel", "parallel"), vmem_limit_bytes=VMEM_LIMIT),
    )(x2, g, *ws, wup, bg)


def _gla_kernel(q_ref, k_ref, v_ref, la_ref, r_ref, gn_ref, s0_ref, o_ref, sf_ref, s_scr,
                *, n_chunks):
    j = pl.program_id(1)

    @pl.when(j == 0)
    def _():
        s_scr[...] = s0_ref[...]

    row = lax.broadcasted_iota(jnp.int32, (CHUNK, CHUNK), 0)
    col = lax.broadcasted_iota(jnp.int32, (CHUNK, CHUNK), 1)
    causal = row >= col
    tril = jnp.where(causal, 1.0, 0.0).astype(BF16)
    ones_t = jnp.ones((CHUNK, LANE), BF16)
    tn = (((0,), (0,)), ((), ()))
    nt = (((1,), (1,)), ((), ()))

    def chunk(c, carry):
        r0 = pl.multiple_of(c * CHUNK, CHUNK)
        rows = pl.ds(r0, CHUNK)
        la = la_ref[rows, :]
        la_hi = la.astype(BF16)
        la_lo = (la - la_hi.astype(F32)).astype(BF16)
        b = _dot(tril, la_hi) + _dot(tril, la_lo)
        b_mid = b[CHUNK // 2:CHUNK // 2 + 1, :]
        b_last = b[CHUNK - 1:CHUNK, :]
        q = q_ref[rows, :].astype(F32)
        k = k_ref[rows, :].astype(F32)
        q_in = (q * jnp.exp(b - b_mid)).astype(BF16)
        k_in = (k * jnp.exp(b_mid - b)).astype(BF16)
        q_inter = (q * jnp.exp(b)).astype(BF16)
        k_state = (k * jnp.exp(b_last - b)).astype(BF16)
        v = v_ref[rows, :]
        for h in range(HEADS):
            ks = slice(h * DK, (h + 1) * DK)
            vs = slice(h * DV, (h + 1) * DV)
            vh = v[:, vs]
            s_old = s_scr[h]
            scores = lax.dot_general(q_in[:, ks], k_in[:, ks], nt, preferred_element_type=F32)
            scores = jnp.where(causal, scores, 0.0).astype(BF16)
            o = _dot(scores, vh) + _dot(q_inter[:, ks], s_old.astype(BF16))
            bl_col = (lax.dot_general(la_hi[:, ks], ones_t, tn, preferred_element_type=F32)
                      + lax.dot_general(la_lo[:, ks], ones_t, tn, preferred_element_type=F32))
            dc = jnp.exp(bl_col)
            dc = jnp.concatenate([dc, dc], axis=1)
            kv = lax.dot_general(k_state[:, ks], vh, tn, preferred_element_type=F32)
            s_scr[h] = dc * s_old + kv
            ms = jnp.mean(o * o, axis=-1, keepdims=True)
            og = o * lax.rsqrt(ms + EPS) * gn_ref[:, vs] * r_ref[rows, vs].astype(F32)
            o_ref[rows, vs] = og.astype(BF16)
        return carry

    lax.fori_loop(0, n_chunks, chunk, 0)

    @pl.when(j == pl.num_programs(1) - 1)
    def _():
        sf_ref[...] = s_scr[...]


def _gla(q, k, v, la, r, gn, s0, *, nb, rows_per_b, chunks_per_step):
    tm = chunks_per_step * CHUNK
    nt = rows_per_b // tm
    row_spec = lambda w: pl.BlockSpec((tm, w), lambda b, i: (b * nt + i, 0))
    state_spec = pl.BlockSpec((None, HEADS, DK, DV), lambda b, i: (b, 0, 0, 0))
    return pl.pallas_call(
        functools.partial(_gla_kernel, n_chunks=chunks_per_step),
        grid=(nb, nt),
        in_specs=[row_spec(KEY), row_spec(KEY), row_spec(VAL), row_spec(KEY), row_spec(VAL),
                  _const_spec(gn.shape), state_spec],
        out_specs=[row_spec(VAL), state_spec],
        out_shape=[jax.ShapeDtypeStruct((nb * rows_per_b, VAL), BF16),
                   jax.ShapeDtypeStruct((nb, HEADS, DK, DV), F32)],
        scratch_shapes=[pltpu.VMEM((HEADS, DK, DV), F32)],
        name="gla",
        compiler_params=pltpu.CompilerParams(
            dimension_semantics=("parallel", "arbitrary"), vmem_limit_bytes=VMEM_LIMIT),
    )(q, k, v, la, r, gn, s0)


def _s5disc_kernel(ar_ref, ai_ref, ls_ref, br_ref, bi_ref, lre_o, lim_o, bbre_o, bbim_o):
    ar = ar_ref[...]
    ai = ai_ref[...]
    dt = jnp.exp(ls_ref[...])
    mag = jnp.exp(ar * dt)
    lre = mag * jnp.cos(ai * dt)
    lim = mag * jnp.sin(ai * dt)
    zr = lre - 1.0
    zi = lim
    den = ar * ar + ai * ai
    fr = (zr * ar + zi * ai) / den
    fi = (zi * ar - zr * ai) / den
    br = br_ref[...]
    bi = bi_ref[...]
    lre_o[...] = lre
    lim_o[...] = lim
    bbre_o[...] = fr * br - fi * bi
    bbim_o[...] = fr * bi + fi * br


def _s5disc(ar, ai, ls, br, bi):
    shp = jax.ShapeDtypeStruct((WIDTH, NSTATE), F32)
    return pl.pallas_call(
        _s5disc_kernel, out_shape=[shp, shp, shp, shp], name="s5disc",
    )(ar, ai, ls, br, bi)


SCAN_TILES = 4


def _s5_kernel(u_ref, bblk_ref, cblk_ref, lre_ref, lim_ref, d_ref, h0_ref, y_ref, hf_ref,
               bu_scr, st_scr, *, steps):
    i = pl.program_id(0)

    @pl.when(i == 0)
    def _():
        st_scr[...] = h0_ref[...]

    for sl in range(N_SLAB):
        cs = slice(sl * SLAB, (sl + 1) * SLAB)
        u = u_ref[:, cs]
        bu_scr[...] = _dot(u, bblk_ref[sl])
        for grp in range(SLAB_STATE // LANE // SCAN_TILES):
            lanes_re = [pl.ds((grp * SCAN_TILES + t) * LANE, LANE) for t in range(SCAN_TILES)]
            lanes_im = [pl.ds(SLAB_STATE + (grp * SCAN_TILES + t) * LANE, LANE)
                        for t in range(SCAN_TILES)]
            lr = [jnp.broadcast_to(lre_ref[sl:sl + 1, l], (SUB, LANE)) for l in lanes_re]
            li = [jnp.broadcast_to(lim_ref[sl:sl + 1, l], (SUB, LANE)) for l in lanes_re]
            s_re0 = tuple(st_scr[sl, :, l] for l in lanes_re)
            s_im0 = tuple(st_scr[sl, :, l] for l in lanes_im)

            def step(t, carry):
                s_re, s_im = carry
                rows = pl.ds(pl.multiple_of(t * SUB, SUB), SUB)
                n_re, n_im = [], []
                for a in range(SCAN_TILES):
                    x_re = lr[a] * s_re[a] - li[a] * s_im[a] + bu_scr[rows, lanes_re[a]]
                    x_im = lr[a] * s_im[a] + li[a] * s_re[a] + bu_scr[rows, lanes_im[a]]
                    bu_scr[rows, lanes_re[a]] = x_re
                    bu_scr[rows, lanes_im[a]] = x_im
                    n_re.append(x_re)
                    n_im.append(x_im)
                return tuple(n_re), tuple(n_im)

            s_re, s_im = lax.fori_loop(0, steps, step, (s_re0, s_im0), unroll=2)
            for a in range(SCAN_TILES):
                st_scr[sl, :, lanes_re[a]] = s_re[a]
                st_scr[sl, :, lanes_im[a]] = s_im[a]
        y = _dot(bu_scr[...].astype(BF16), cblk_ref[sl])
        y_ref[:, cs] = (y + d_ref[:, cs] * u.astype(F32)).astype(BF16)

    @pl.when(i == pl.num_programs(0) - 1)
    def _():
        hf_ref[...] = st_scr[...]


def _s5(u2, bblk, cblk, lre, lim, d, h0, *, steps):
    rows = u2.shape[0]
    tm = steps * SUB
    row_spec = pl.BlockSpec((tm, WIDTH), lambda i: (i, 0))
    return pl.pallas_call(
        functools.partial(_s5_kernel, steps=steps),
        grid=(rows // tm,),
        in_specs=[row_spec, _const_spec(bblk.shape), _const_spec(cblk.shape),
                  _const_spec(lre.shape), _const_spec(lim.shape), _const_spec(d.shape),
                  _const_spec(h0.shape)],
        out_specs=[row_spec, _const_spec(h0.shape)],
        out_shape=[jax.ShapeDtypeStruct((rows, WIDTH), BF16),
                   jax.ShapeDtypeStruct(h0.shape, F32)],
        scratch_shapes=[pltpu.VMEM((tm, 2 * SLAB_STATE), F32),
                        pltpu.VMEM(h0.shape, F32)],
        name="s5",
        compiler_params=pltpu.CompilerParams(
            dimension_semantics=("arbitrary",), vmem_limit_bytes=VMEM_LIMIT),
    )(u2, bblk, cblk, lre, lim, d, h0)


FF_SPLIT = 4


def _post_kernel(x_ref, og_ref, ys_ref, sg_ref, ss_ref, wo, wglu, bglu, wout, gpost, gffn,
                 w1, w2, gfpost, out_ref):
    y_gla = _dot(og_ref[...], wo[...])
    ys = ys_ref[...].astype(F32)
    gs = 0.5 * ys * (1.0 + jnp.tanh(0.7978845608028654 * (ys + 0.044715 * (ys * ys * ys))))
    glu = _dot(gs.astype(BF16), wglu[...]) + bglu[...]
    y_ssm = glu[:, :D_MODEL] * _sigmoid(glu[:, D_MODEL:])
    mixed = sg_ref[...].astype(F32) * y_gla + ss_ref[...].astype(F32) * y_ssm
    h1 = x_ref[...] + _rms(_dot(mixed.astype(BF16), wout[...]), gpost[...])
    hn = _rms(h1, gffn[...]).astype(BF16)
    wcols = D_FF // FF_SPLIT
    f = None
    for c in range(FF_SPLIT):
        a = jnp.maximum(_dot(hn, w1[:, c * wcols:(c + 1) * wcols]), 0.0)
        part = _dot((a * a).astype(BF16), w2[c * wcols:(c + 1) * wcols, :])
        f = part if f is None else f + part
    out_ref[...] = h1 + _rms(f, gfpost[...])


def _post(x2, og, ys_tb, sg, ss, consts, *, nb, rows_per_b, tm):
    nt = rows_per_b // tm
    row_spec = lambda w: pl.BlockSpec((tm, w), lambda b, i: (b * nt + i, 0))
    ys_spec = pl.BlockSpec((tm, WIDTH), lambda b, i: (i, b))
    in_specs = [row_spec(D_MODEL), row_spec(VAL), ys_spec, row_spec(D_MODEL), row_spec(D_MODEL)]
    in_specs += [_const_spec(c.shape) for c in consts]
    return pl.pallas_call(
        _post_kernel, grid=(nb, nt), in_specs=in_specs, out_specs=row_spec(D_MODEL),
        out_shape=jax.ShapeDtypeStruct((nb * rows_per_b, D_MODEL), F32), name="post",
        compiler_params=pltpu.CompilerParams(
            dimension_semantics=("parallel", "parallel"), vmem_limit_bytes=VMEM_LIMIT),
    )(x2, og, ys_tb, sg, ss, *consts)


def _block_diag(m):
    eye = jnp.eye(SLAB_GROUPS, dtype=m.dtype)
    out = jnp.einsum("qgab,gk->qgakb", m, eye)
    return out.reshape(N_SLAB, SLAB_GROUPS * m.shape[2], SLAB_GROUPS * m.shape[3])


def kernel(x, meta_tokens, g_mix_pre, w_in, w_gate_up, b_gate, gla_norm_g, w_o_gla, a_re, a_im,
           log_step, b_re, b_im, c_re, c_im, d_skip, w_glu, b_glu, w_out, g_mix_post, g_ffn_pre,
           w_ff1, w_ff2, g_ffn_post):
    bsz, seq, dm = x.shape
    assert dm == D_MODEL and bsz == SUB and seq % CHUNK == 0
    assert w_in.shape[0] == 1, "single layer"
    l = 0
    row = lambda t: t.reshape(1, -1).astype(F32)

    w = w_in[l]
    off = [0, KEY, 2 * KEY, 2 * KEY + VAL, 2 * KEY + 2 * VAL]
    wq = w[:, off[0]:off[1]].astype(BF16)
    wk = w[:, off[1]:off[2]].astype(BF16)
    wv = w[:, off[2]:off[3]].astype(BF16)
    wr = w[:, off[3]:off[4]].astype(BF16)
    a0 = off[4]
    wa = jnp.pad(w[:, a0:a0 + RANK], ((0, 0), (0, LANE - RANK))).astype(BF16)
    u0 = a0 + RANK
    wu = w[:, u0:u0 + WIDTH].astype(BF16)
    wzg = w[:, u0 + WIDTH:u0 + WIDTH + D_MODEL].astype(BF16)
    wzs = w[:, u0 + WIDTH + D_MODEL:].astype(BF16)
    ws = (wq, wk, wv, wr, wa, wu, wzg, wzs)
    wup = jnp.pad(w_gate_up[l], ((0, LANE - RANK), (0, 0))).astype(BF16)
    bg = row(b_gate[l])
    gpre = row(g_mix_pre[l])
    gn = row(gla_norm_g[l])

    rep = lambda t: jnp.repeat(t.astype(F32), GSIZE, axis=0)
    br_t = jnp.swapaxes(b_re[l], 1, 2).reshape(WIDTH, NSTATE).astype(F32)
    bi_t = jnp.swapaxes(b_im[l], 1, 2).reshape(WIDTH, NSTATE).astype(F32)
    lre_w, lim_w, bbre, bbim = _s5disc(rep(a_re[l]), rep(a_im[l]), rep(log_step[l][:, None]),
                                       br_t, bi_t)
    lam_re = lre_w[::GSIZE].reshape(N_SLAB, SLAB_STATE)
    lam_im = lim_w[::GSIZE].reshape(N_SLAB, SLAB_STATE)
    grp4 = lambda t, a, b: t.reshape(N_SLAB, SLAB_GROUPS, a, b)
    bblk = jnp.concatenate([_block_diag(grp4(bbre, GSIZE, NSTATE)),
                            _block_diag(grp4(bbim, GSIZE, NSTATE))], axis=2).astype(BF16)
    cr_t = jnp.swapaxes(c_re[l].astype(F32), 1, 2)
    ci_t = jnp.swapaxes(c_im[l].astype(F32), 1, 2)
    cblk = jnp.concatenate([_block_diag(grp4(cr_t, NSTATE, GSIZE)),
                            _block_diag(grp4(-ci_t, NSTATE, GSIZE))], axis=1).astype(BF16)
    dsk = row(d_skip[l])

    mq, mk, mv, mr, mla, mu, _, _ = _inproj(
        meta_tokens.astype(F32), gpre, ws, wup, bg, nb=1, rows_per_b=N_META, tm=N_META,
        u_time_major=False)
    padr = lambda t: jnp.pad(t, ((N_PAD, 0), (0, 0)))
    s_zero = jnp.zeros((1, HEADS, DK, DV), F32)
    _, s_meta = _gla(padr(mq), padr(mk), padr(mv), padr(mla), padr(mr), gn, s_zero,
                     nb=1, rows_per_b=CHUNK, chunks_per_step=1)
    mu_tb = jnp.broadcast_to(padr(mu)[:, None, :], (CHUNK, SUB, WIDTH)).reshape(CHUNK * SUB, WIDTH)
    h_zero = jnp.zeros((N_SLAB, SUB, 2 * SLAB_STATE), F32)
    _, h_meta = _s5(mu_tb, bblk, cblk, lam_re, lam_im, dsk, h_zero, steps=CHUNK)

    x2 = x.reshape(bsz * seq, dm)
    tm = 512 if seq % 512 == 0 else CHUNK
    q, k, v, r, la, u_tb, sg, ss = _inproj(x2, gpre, ws, wup, bg, nb=bsz, rows_per_b=seq, tm=tm,
                                           u_time_major=True)
    s0 = jnp.broadcast_to(s_meta, (bsz, HEADS, DK, DV))
    og, _ = _gla(q, k, v, la, r, gn, s0, nb=bsz, rows_per_b=seq, chunks_per_step=tm // CHUNK)
    ys2, _ = _s5(u_tb.reshape(seq * bsz, WIDTH), bblk, cblk, lam_re, lam_im, dsk, h_meta,
                 steps=CHUNK)
    consts = (w_o_gla[l].astype(BF16), w_glu[l].astype(BF16), row(b_glu[l]), w_out[l].astype(BF16),
              row(g_mix_post[l]), row(g_ffn_pre[l]), w_ff1[l].astype(BF16), w_ff2[l].astype(BF16),
              row(g_ffn_post[l]))
    out = _post(x2, og, ys2.reshape(seq, bsz * WIDTH), sg, ss, consts, nb=bsz, rows_per_b=seq, tm=tm)
    return out.reshape(bsz, seq, dm)
```

```python
import functools

import jax
import jax.numpy as jnp
from jax import lax
from jax.experimental import pallas as pl
from jax.experimental.pallas import tpu as pltpu

F32 = jnp.float32
BF16 = jnp.bfloat16

D_MODEL = 1024
N_META = 16
CHUNK = 64
N_PAD = CHUNK - N_META
HEADS = 4
DK = 128
DV = 256
KEY = HEADS * DK
VAL = HEADS * DV
RANK = 16
TAU = 16.0
GROUPS = 64
GSIZE = 16
NSTATE = 64
WIDTH = GROUPS * GSIZE
D_FF = 4 * D_MODEL
EPS = 1e-6

LANE = 128
SUB = 8
SLAB = 256
SLAB_GROUPS = SLAB // GSIZE
N_SLAB = WIDTH // SLAB
SLAB_STATE = SLAB_GROUPS * NSTATE
VMEM_LIMIT = 56 * 1024 * 1024


def _const_spec(shape):
    nd = len(shape)
    return pl.BlockSpec(shape, lambda *_: (0,) * nd, pipeline_mode=pl.Buffered(1))


def _rms(x, g):
    ms = jnp.mean(x * x, axis=-1, keepdims=True)
    return x * lax.rsqrt(ms + EPS) * g


def _sigmoid(x):
    return 1.0 / (1.0 + jnp.exp(-x))


def _dot(a, b):
    return jnp.dot(a, b, preferred_element_type=F32)


def _inproj_kernel(x_ref, g_ref, wq, wk, wv, wr, wa, wu, wzg, wzs, wup, bg,
                   q_o, k_o, v_o, r_o, la_o, u_o, zg_o, zs_o):
    xn = _rms(x_ref[...], g_ref[...]).astype(BF16)

    def mm(w):
        return _dot(xn, w[...])

    q_o[...] = (mm(wq) * (DK ** -0.5)).astype(BF16)
    k_o[...] = mm(wk).astype(BF16)
    v_o[...] = mm(wv).astype(BF16)
    r = mm(wr)
    r_o[...] = (r * _sigmoid(r)).astype(BF16)
    z = _dot(mm(wa).astype(BF16), wup[...]) + bg[...]
    la_o[...] = (jnp.minimum(z, 0.0) - jnp.log(1.0 + jnp.exp(-jnp.abs(z)))) * (1.0 / TAU)
    u_o[...] = mm(wu).astype(BF16)
    zg_o[...] = _sigmoid(mm(wzg)).astype(BF16)
    zs_o[...] = _sigmoid(mm(wzs)).astype(BF16)


def _inproj(x2, g, ws, wup, bg, *, nb, rows_per_b, tm, u_time_major):
    rows = nb * rows_per_b
    nt = rows_per_b // tm
    row_spec = lambda w: pl.BlockSpec((tm, w), lambda b, i: (b * nt + i, 0))
    if u_time_major:
        u_shape = jax.ShapeDtypeStruct((rows_per_b, nb * WIDTH), BF16)
        u_spec = pl.BlockSpec((tm, WIDTH), lambda b, i: (i, b))
    else:
        u_shape = jax.ShapeDtypeStruct((rows, WIDTH), BF16)
        u_spec = row_spec(WIDTH)
    out_shape = [
        jax.ShapeDtypeStruct((rows, KEY), BF16), jax.ShapeDtypeStruct((rows, KEY), BF16),
        jax.ShapeDtypeStruct((rows, VAL), BF16), jax.ShapeDtypeStruct((rows, VAL), BF16),
        jax.ShapeDtypeStruct((rows, KEY), F32), u_shape,
        jax.ShapeDtypeStruct((rows, D_MODEL), BF16), jax.ShapeDtypeStruct((rows, D_MODEL), BF16),
    ]
    out_specs = [row_spec(KEY), row_spec(KEY), row_spec(VAL), row_spec(VAL), row_spec(KEY),
                 u_spec, row_spec(D_MODEL), row_spec(D_MODEL)]
    in_specs = [row_spec(D_MODEL), _const_spec(g.shape)]
    in_specs += [_const_spec(w.shape) for w in ws]
    in_specs += [_const_spec(wup.shape), _const_spec(bg.shape)]
    return pl.pallas_call(
        _inproj_kernel, grid=(nb, nt), in_specs=in_specs, out_specs=out_specs,
        out_shape=out_shape, name="inproj",
        compiler_params=pltpu.CompilerParams(
            dimension_semantics=("parallel", "parallel"), vmem_limit_bytes=VMEM_LIMIT),
    )(x2, g, *ws, wup, bg)


def _gla_kernel(q_ref, k_ref, v_ref, la_ref, r_ref, gn_ref, s0_ref, o_ref, sf_ref,
                s_scr, b_scr, qi_scr, ki_scr, qs_scr, ks_scr, dc_scr, kv_scr, sc_scr, sb_scr,
                *, n_chunks):
    j = pl.program_id(1)

    @pl.when(j == 0)
    def _():
        s_scr[...] = s0_ref[...]

    row = lax.broadcasted_iota(jnp.int32, (CHUNK, CHUNK), 0)
    col = lax.broadcasted_iota(jnp.int32, (CHUNK, CHUNK), 1)
    causal = row >= col
    tril = jnp.where(causal, 1.0, 0.0).astype(BF16)
    tn = (((0,), (0,)), ((), ()))
    nt = (((1,), (1,)), ((), ()))

    crow = lambda c: slice(c * CHUNK, (c + 1) * CHUNK)
    ksl = lambda h: slice(h * DK, (h + 1) * DK)
    vsl = lambda h: slice(h * DV, (h + 1) * DV)

    for c in range(n_chunks):
        la = la_ref[crow(c), :]
        la_hi = la.astype(BF16)
        la_lo = (la - la_hi.astype(F32)).astype(BF16)
        b_scr[crow(c), :] = _dot(tril, la_hi) + _dot(tril, la_lo)

    for c in range(n_chunks):
        b = b_scr[crow(c), :]
        b_mid = b[CHUNK // 2:CHUNK // 2 + 1, :]
        b_last = b[CHUNK - 1:CHUNK, :]
        q_in = q_ref[crow(c), :].astype(F32) * jnp.exp(b - b_mid)
        k_in = k_ref[crow(c), :].astype(F32) * jnp.exp(b_mid - b)
        qs_scr[crow(c), :] = (q_in * jnp.exp(b_mid)).astype(BF16)
        ks_scr[crow(c), :] = (k_in * jnp.exp(b_last - b_mid)).astype(BF16)
        qi_scr[crow(c), :] = q_in.astype(BF16)
        ki_scr[crow(c), :] = k_in.astype(BF16)
        dc_scr[c] = jnp.broadcast_to(jnp.exp(b_last), (SUB, KEY))

    for c in range(n_chunks):
        for h in range(HEADS):
            kv_scr[c, h] = lax.dot_general(ks_scr[crow(c), ksl(h)], v_ref[crow(c), vsl(h)], tn,
                                           preferred_element_type=F32)
    for c in range(n_chunks):
        for h in range(HEADS):
            sc = lax.dot_general(qi_scr[crow(c), ksl(h)], ki_scr[crow(c), ksl(h)], nt,
                                 preferred_element_type=F32)
            sc_scr[c, h] = jnp.where(causal, sc, 0.0).astype(BF16)

    for c in range(n_chunks):
        for h in range(HEADS):
            s_old = s_scr[h]
            sb_scr[c, h] = s_old.astype(BF16)
            dcb = jnp.broadcast_to(dc_scr[c, 0:1, ksl(h)], (DK, DK)).T
            s_scr[h] = jnp.concatenate([dcb, dcb], axis=1) * s_old + kv_scr[c, h]

    for c in range(n_chunks):
        for h in range(HEADS):
            lhs = jnp.concatenate([qs_scr[crow(c), ksl(h)], sc_scr[c, h]], axis=1)
            rhs = jnp.concatenate([sb_scr[c, h], v_ref[crow(c), vsl(h)]], axis=0)
            o = _dot(lhs, rhs)
            ms = jnp.mean(o * o, axis=-1, keepdims=True)
            og = o * lax.rsqrt(ms + EPS) * gn_ref[:, vsl(h)] * r_ref[crow(c), vsl(h)].astype(F32)
            o_ref[crow(c), vsl(h)] = og.astype(BF16)

    @pl.when(j == pl.num_programs(1) - 1)
    def _():
        sf_ref[...] = s_scr[...]


def _gla(q, k, v, la, r, gn, s0, *, nb, rows_per_b, chunks_per_step):
    nc = chunks_per_step
    tm = nc * CHUNK
    nt = rows_per_b // tm
    row_spec = lambda w: pl.BlockSpec((tm, w), lambda b, i: (b * nt + i, 0))
    state_spec = pl.BlockSpec((None, HEADS, DK, DV), lambda b, i: (b, 0, 0, 0))
    return pl.pallas_call(
        functools.partial(_gla_kernel, n_chunks=nc),
        grid=(nb, nt),
        in_specs=[row_spec(KEY), row_spec(KEY), row_spec(VAL), row_spec(KEY), row_spec(VAL),
                  _const_spec(gn.shape), state_spec],
        out_specs=[row_spec(VAL), state_spec],
        out_shape=[jax.ShapeDtypeStruct((nb * rows_per_b, VAL), BF16),
                   jax.ShapeDtypeStruct((nb, HEADS, DK, DV), F32)],
        scratch_shapes=[pltpu.VMEM((HEADS, DK, DV), F32),
                        pltpu.VMEM((tm, KEY), F32),
                        pltpu.VMEM((tm, KEY), BF16), pltpu.VMEM((tm, KEY), BF16),
                        pltpu.VMEM((tm, KEY), BF16), pltpu.VMEM((tm, KEY), BF16),
                        pltpu.VMEM((nc, SUB, KEY), F32),
                        pltpu.VMEM((nc, HEADS, DK, DV), F32),
                        pltpu.VMEM((nc, HEADS, CHUNK, CHUNK), BF16),
                        pltpu.VMEM((nc, HEADS, DK, DV), BF16)],
        name="gla",
        compiler_params=pltpu.CompilerParams(
            dimension_semantics=("parallel", "arbitrary"), vmem_limit_bytes=VMEM_LIMIT),
    )(q, k, v, la, r, gn, s0)


def _s5disc_kernel(ar_ref, ai_ref, ls_ref, br_ref, bi_ref, lre_o, lim_o, bbre_o, bbim_o):
    ar = ar_ref[...]
    ai = ai_ref[...]
    dt = jnp.exp(ls_ref[...])
    mag = jnp.exp(ar * dt)
    lre = mag * jnp.cos(ai * dt)
    lim = mag * jnp.sin(ai * dt)
    zr = lre - 1.0
    zi = lim
    den = ar * ar + ai * ai
    fr = (zr * ar + zi * ai) / den
    fi = (zi * ar - zr * ai) / den
    br = br_ref[...]
    bi = bi_ref[...]
    lre_o[...] = lre
    lim_o[...] = lim
    bbre_o[...] = fr * br - fi * bi
    bbim_o[...] = fr * bi + fi * br


def _s5disc(ar, ai, ls, br, bi):
    shp = jax.ShapeDtypeStruct((WIDTH, NSTATE), F32)
    return pl.pallas_call(
        _s5disc_kernel, out_shape=[shp, shp, shp, shp], name="s5disc",
    )(ar, ai, ls, br, bi)


SCAN_TILES = 4


def _s5_kernel(u_ref, bblk_ref, cblk_ref, lre_ref, lim_ref, d_ref, h0_ref, y_ref, hf_ref,
               bu_scr, st_scr, *, steps):
    i = pl.program_id(0)

    @pl.when(i == 0)
    def _():
        st_scr[...] = h0_ref[...]

    for sl in range(N_SLAB):
        cs = slice(sl * SLAB, (sl + 1) * SLAB)
        u = u_ref[:, cs]
        bu_scr[...] = _dot(u, bblk_ref[sl])
        for grp in range(SLAB_STATE // LANE // SCAN_TILES):
            lanes_re = [pl.ds((grp * SCAN_TILES + t) * LANE, LANE) for t in range(SCAN_TILES)]
            lanes_im = [pl.ds(SLAB_STATE + (grp * SCAN_TILES + t) * LANE, LANE)
                        for t in range(SCAN_TILES)]
            lr = [jnp.broadcast_to(lre_ref[sl:sl + 1, l], (SUB, LANE)) for l in lanes_re]
            li = [jnp.broadcast_to(lim_ref[sl:sl + 1, l], (SUB, LANE)) for l in lanes_re]
            s_re0 = tuple(st_scr[sl, :, l] for l in lanes_re)
            s_im0 = tuple(st_scr[sl, :, l] for l in lanes_im)

            def step(t, carry):
                s_re, s_im = carry
                rows = pl.ds(pl.multiple_of(t * SUB, SUB), SUB)
                n_re, n_im = [], []
                for a in range(SCAN_TILES):
                    x_re = lr[a] * s_re[a] - li[a] * s_im[a] + bu_scr[rows, lanes_re[a]]
                    x_im = lr[a] * s_im[a] + li[a] * s_re[a] + bu_scr[rows, lanes_im[a]]
                    bu_scr[rows, lanes_re[a]] = x_re
                    bu_scr[rows, lanes_im[a]] = x_im
                    n_re.append(x_re)
                    n_im.append(x_im)
                return tuple(n_re), tuple(n_im)

            s_re, s_im = lax.fori_loop(0, steps, step, (s_re0, s_im0), unroll=2)
            for a in range(SCAN_TILES):
                st_scr[sl, :, lanes_re[a]] = s_re[a]
                st_scr[sl, :, lanes_im[a]] = s_im[a]
        y = _dot(bu_scr[...].astype(BF16), cblk_ref[sl])
        y_ref[:, cs] = (y + d_ref[:, cs] * u.astype(F32)).astype(BF16)

    @pl.when(i == pl.num_programs(0) - 1)
    def _():
        hf_ref[...] = st_scr[...]


def _s5(u2, bblk, cblk, lre, lim, d, h0, *, steps):
    rows = u2.shape[0]
    tm = steps * SUB
    row_spec = pl.BlockSpec((tm, WIDTH), lambda i: (i, 0))
    return pl.pallas_call(
        functools.partial(_s5_kernel, steps=steps),
        grid=(rows // tm,),
        in_specs=[row_spec, _const_spec(bblk.shape), _const_spec(cblk.shape),
                  _const_spec(lre.shape), _const_spec(lim.shape), _const_spec(d.shape),
                  _const_spec(h0.shape)],
        out_specs=[row_spec, _const_spec(h0.shape)],
        out_shape=[jax.ShapeDtypeStruct((rows, WIDTH), BF16),
                   jax.ShapeDtypeStruct(h0.shape, F32)],
        scratch_shapes=[pltpu.VMEM((tm, 2 * SLAB_STATE), F32),
                        pltpu.VMEM(h0.shape, F32)],
        name="s5",
        compiler_params=pltpu.CompilerParams(
            dimension_semantics=("arbitrary",), vmem_limit_bytes=VMEM_LIMIT),
    )(u2, bblk, cblk, lre, lim, d, h0)


FF_SPLIT = 4


def _post_kernel(x_ref, og_ref, ys_ref, sg_ref, ss_ref, wo, wglu, bglu, wout, gpost, gffn,
                 w1, w2, gfpost, out_ref):
    y_gla = _dot(og_ref[...], wo[...])
    ys = ys_ref[...].astype(F32)
    gs = 0.5 * ys * (1.0 + jnp.tanh(0.7978845608028654 * (ys + 0.044715 * (ys * ys * ys))))
    glu = _dot(gs.astype(BF16), wglu[...]) + bglu[...]
    y_ssm = glu[:, :D_MODEL] * _sigmoid(glu[:, D_MODEL:])
    mixed = sg_ref[...].astype(F32) * y_gla + ss_ref[...].astype(F32) * y_ssm
    h1 = x_ref[...] + _rms(_dot(mixed.astype(BF16), wout[...]), gpost[...])
    hn = _rms(h1, gffn[...]).astype(BF16)
    wcols = D_FF // FF_SPLIT
    f = None
    for c in range(FF_SPLIT):
        a = jnp.maximum(_dot(hn, w1[:, c * wcols:(c + 1) * wcols]), 0.0)
        part = _dot((a * a).astype(BF16), w2[c * wcols:(c + 1) * wcols, :])
        f = part if f is None else f + part
    out_ref[...] = h1 + _rms(f, gfpost[...])


def _post(x2, og, ys_tb, sg, ss, consts, *, nb, rows_per_b, tm):
    nt = rows_per_b // tm
    row_spec = lambda w: pl.BlockSpec((tm, w), lambda b, i: (b * nt + i, 0))
    ys_spec = pl.BlockSpec((tm, WIDTH), lambda b, i: (i, b))
    in_specs = [row_spec(D_MODEL), row_spec(VAL), ys_spec, row_spec(D_MODEL), row_spec(D_MODEL)]
    in_specs += [_const_spec(c.shape) for c in consts]
    return pl.pallas_call(
        _post_kernel, grid=(nb, nt), in_specs=in_specs, out_specs=row_spec(D_MODEL),
        out_shape=jax.ShapeDtypeStruct((nb * rows_per_b, D_MODEL), F32), name="post",
        compiler_params=pltpu.CompilerParams(
            dimension_semantics=("parallel", "parallel"), vmem_limit_bytes=VMEM_LIMIT),
    )(x2, og, ys_tb, sg, ss, *consts)


def _block_diag(m):
    eye = jnp.eye(SLAB_GROUPS, dtype=m.dtype)
    out = jnp.einsum("qgab,gk->qgakb", m, eye)
    return out.reshape(N_SLAB, SLAB_GROUPS * m.shape[2], SLAB_GROUPS * m.shape[3])


def kernel(x, meta_tokens, g_mix_pre, w_in, w_gate_up, b_gate, gla_norm_g, w_o_gla, a_re, a_im,
           log_step, b_re, b_im, c_re, c_im, d_skip, w_glu, b_glu, w_out, g_mix_post, g_ffn_pre,
           w_ff1, w_ff2, g_ffn_post):
    bsz, seq, dm = x.shape
    assert dm == D_MODEL and bsz == SUB and seq % CHUNK == 0
    assert w_in.shape[0] == 1, "single layer"
    l = 0
    row = lambda t: t.reshape(1, -1).astype(F32)

    w = w_in[l]
    off = [0, KEY, 2 * KEY, 2 * KEY + VAL, 2 * KEY + 2 * VAL]
    wq = w[:, off[0]:off[1]].astype(BF16)
    wk = w[:, off[1]:off[2]].astype(BF16)
    wv = w[:, off[2]:off[3]].astype(BF16)
    wr = w[:, off[3]:off[4]].astype(BF16)
    a0 = off[4]
    wa = jnp.pad(w[:, a0:a0 + RANK], ((0, 0), (0, LANE - RANK))).astype(BF16)
    u0 = a0 + RANK
    wu = w[:, u0:u0 + WIDTH].astype(BF16)
    wzg = w[:, u0 + WIDTH:u0 + WIDTH + D_MODEL].astype(BF16)
    wzs = w[:, u0 + WIDTH + D_MODEL:].astype(BF16)
    ws = (wq, wk, wv, wr, wa, wu, wzg, wzs)
    wup = jnp.pad(w_gate_up[l], ((0, LANE - RANK), (0, 0))).astype(BF16)
    bg = row(b_gate[l])
    gpre = row(g_mix_pre[l])
    gn = row(gla_norm_g[l])

    rep = lambda t: jnp.repeat(t.astype(F32), GSIZE, axis=0)
    br_t = jnp.swapaxes(b_re[l], 1, 2).reshape(WIDTH, NSTATE).astype(F32)
    bi_t = jnp.swapaxes(b_im[l], 1, 2).reshape(WIDTH, NSTATE).astype(F32)
    lre_w, lim_w, bbre, bbim = _s5disc(rep(a_re[l]), rep(a_im[l]), rep(log_step[l][:, None]),
                                       br_t, bi_t)
    lam_re = lre_w[::GSIZE].reshape(N_SLAB, SLAB_STATE)
    lam_im = lim_w[::GSIZE].reshape(N_SLAB, SLAB_STATE)
    grp4 = lambda t, a, b: t.reshape(N_SLAB, SLAB_GROUPS, a, b)
    bblk = jnp.concatenate([_block_diag(grp4(bbre, GSIZE, NSTATE)),
                            _block_diag(grp4(bbim, GSIZE, NSTATE))], axis=2).astype(BF16)
    cr_t = jnp.swapaxes(c_re[l].astype(F32), 1, 2)
    ci_t = jnp.swapaxes(c_im[l].astype(F32), 1, 2)
    cblk = jnp.concatenate([_block_diag(grp4(cr_t, NSTATE, GSIZE)),
                            _block_diag(grp4(-ci_t, NSTATE, GSIZE))], axis=1).astype(BF16)
    dsk = row(d_skip[l])

    mq, mk, mv, mr, mla, mu, _, _ = _inproj(
        meta_tokens.astype(F32), gpre, ws, wup, bg, nb=1, rows_per_b=N_META, tm=N_META,
        u_time_major=False)
    padr = lambda t: jnp.pad(t, ((N_PAD, 0), (0, 0)))
    s_zero = jnp.zeros((1, HEADS, DK, DV), F32)
    _, s_meta = _gla(padr(mq), padr(mk), padr(mv), padr(mla), padr(mr), gn, s_zero,
                     nb=1, rows_per_b=CHUNK, chunks_per_step=1)
    mu_tb = jnp.broadcast_to(padr(mu)[:, None, :], (CHUNK, SUB, WIDTH)).reshape(CHUNK * SUB, WIDTH)
    h_zero = jnp.zeros((N_SLAB, SUB, 2 * SLAB_STATE), F32)
    _, h_meta = _s5(mu_tb, bblk, cblk, lam_re, lam_im, dsk, h_zero, steps=CHUNK)

    x2 = x.reshape(bsz * seq, dm)
    tm = 512 if seq % 512 == 0 else CHUNK
    q, k, v, r, la, u_tb, sg, ss = _inproj(x2, gpre, ws, wup, bg, nb=bsz, rows_per_b=seq, tm=tm,
                                           u_time_major=True)
    s0 = jnp.broadcast_to(s_meta, (bsz, HEADS, DK, DV))
    og, _ = _gla(q, k, v, la, r, gn, s0, nb=bsz, rows_per_b=seq, chunks_per_step=tm // CHUNK)
    ys2, _ = _s5(u_tb.reshape(seq * bsz, WIDTH), bblk, cblk, lam_re, lam_im, dsk, h_meta,
                 steps=CHUNK)
    consts = (w_o_gla[l].astype(BF16), w_glu[l].astype(BF16), row(b_glu[l]), w_out[l].astype(BF16),
              row(g_mix_post[l]), row(g_ffn_pre[l]), w_ff1[l].astype(BF16), w_ff2[l].astype(BF16),
              row(g_ffn_post[l]))
    out = _post(x2, og, ys2.reshape(seq, bsz * WIDTH), sg, ss, consts, nb=bsz, rows_per_b=seq, tm=tm)
    return out.reshape(bsz, seq, dm)
```

```python
import functools

import jax
import jax.numpy as jnp
from jax import lax
from jax.experimental import pallas as pl
from jax.experimental.pallas import tpu as pltpu

F32 = jnp.float32
BF16 = jnp.bfloat16

D_MODEL = 1024
N_META = 16
CHUNK = 64
N_PAD = CHUNK - N_META
HEADS = 4
DK = 128
DV = 256
KEY = HEADS * DK
VAL = HEADS * DV
RANK = 16
TAU = 16.0
GROUPS = 64
GSIZE = 16
NSTATE = 64
WIDTH = GROUPS * GSIZE
D_FF = 4 * D_MODEL
EPS = 1e-6

LANE = 128
SUB = 8
SLAB = 256
SLAB_GROUPS = SLAB // GSIZE
N_SLAB = WIDTH // SLAB
SLAB_STATE = SLAB_GROUPS * NSTATE
VMEM_LIMIT = 56 * 1024 * 1024


def _const_spec(shape):
    nd = len(shape)
    return pl.BlockSpec(shape, lambda *_: (0,) * nd, pipeline_mode=pl.Buffered(1))


def _rms(x, g):
    ms = jnp.mean(x * x, axis=-1, keepdims=True)
    return x * lax.rsqrt(ms + EPS) * g


def _sigmoid(x):
    return 1.0 / (1.0 + jnp.exp(-x))


def _dot(a, b):
    return jnp.dot(a, b, preferred_element_type=F32)


def _inproj_kernel(x_ref, g_ref, wq, wk, wv, wr, wa, wu, wzg, wzs, wup, bg,
                   q_o, k_o, v_o, r_o, la_o, u_o, zg_o, zs_o):
    xn = _rms(x_ref[...], g_ref[...]).astype(BF16)

    def mm(w):
        return _dot(xn, w[...])

    q_o[...] = (mm(wq) * (DK ** -0.5)).astype(BF16)
    k_o[...] = mm(wk).astype(BF16)
    v_o[...] = mm(wv).astype(BF16)
    r = mm(wr)
    r_o[...] = (r * _sigmoid(r)).astype(BF16)
    z = _dot(mm(wa).astype(BF16), wup[...]) + bg[...]
    la_o[...] = (jnp.minimum(z, 0.0) - jnp.log(1.0 + jnp.exp(-jnp.abs(z)))) * (1.0 / TAU)
    u_o[...] = mm(wu).astype(BF16)
    zg_o[...] = _sigmoid(mm(wzg)).astype(BF16)
    zs_o[...] = _sigmoid(mm(wzs)).astype(BF16)


def _inproj(x2, g, ws, wup, bg, *, nb, rows_per_b, tm):
    rows = nb * rows_per_b
    nt = rows_per_b // tm
    row_spec = lambda w: pl.BlockSpec((tm, w), lambda b, i: (b * nt + i, 0))
    out_shape = [
        jax.ShapeDtypeStruct((rows, KEY), BF16), jax.ShapeDtypeStruct((rows, KEY), BF16),
        jax.ShapeDtypeStruct((rows, VAL), BF16), jax.ShapeDtypeStruct((rows, VAL), BF16),
        jax.ShapeDtypeStruct((rows, KEY), F32), jax.ShapeDtypeStruct((rows, WIDTH), BF16),
        jax.ShapeDtypeStruct((rows, D_MODEL), BF16), jax.ShapeDtypeStruct((rows, D_MODEL), BF16),
    ]
    out_specs = [row_spec(KEY), row_spec(KEY), row_spec(VAL), row_spec(VAL), row_spec(KEY),
                 row_spec(WIDTH), row_spec(D_MODEL), row_spec(D_MODEL)]
    in_specs = [row_spec(D_MODEL), _const_spec(g.shape)]
    in_specs += [_const_spec(w.shape) for w in ws]
    in_specs += [_const_spec(wup.shape), _const_spec(bg.shape)]
    return pl.pallas_call(
        _inproj_kernel, grid=(nb, nt), in_specs=in_specs, out_specs=out_specs,
        out_shape=out_shape, name="inproj",
        compiler_params=pltpu.CompilerParams(
            dimension_semantics=("parallel", "parallel"), vmem_limit_bytes=VMEM_LIMIT),
    )(x2, g, *ws, wup, bg)


def _gla_kernel(q_ref, k_ref, v_ref, la_ref, r_ref, gn_ref, s0_ref, o_ref, sf_ref,
                s_scr, b_scr, qi_scr, ki_scr, qs_scr, ks_scr, dc_scr, kv_scr, sc_scr, sb_scr,
                *, n_chunks):
    j = pl.program_id(1)

    @pl.when(j == 0)
    def _():
        s_scr[...] = s0_ref[...]

    row = lax.broadcasted_iota(jnp.int32, (CHUNK, CHUNK), 0)
    col = lax.broadcasted_iota(jnp.int32, (CHUNK, CHUNK), 1)
    causal = row >= col
    tril = jnp.where(causal, 1.0, 0.0).astype(BF16)
    tn = (((0,), (0,)), ((), ()))
    nt = (((1,), (1,)), ((), ()))

    crow = lambda c: slice(c * CHUNK, (c + 1) * CHUNK)
    ksl = lambda h: slice(h * DK, (h + 1) * DK)
    vsl = lambda h: slice(h * DV, (h + 1) * DV)

    for c in range(n_chunks):
        la = la_ref[crow(c), :]
        la_hi = la.astype(BF16)
        la_lo = (la - la_hi.astype(F32)).astype(BF16)
        b_scr[crow(c), :] = _dot(tril, la_hi) + _dot(tril, la_lo)

    for c in range(n_chunks):
        b = b_scr[crow(c), :]
        b_mid = b[CHUNK // 2:CHUNK // 2 + 1, :]
        b_last = b[CHUNK - 1:CHUNK, :]
        q_in = q_ref[crow(c), :].astype(F32) * jnp.exp(b - b_mid)
        k_in = k_ref[crow(c), :].astype(F32) * jnp.exp(b_mid - b)
        qs_scr[crow(c), :] = (q_in * jnp.exp(b_mid)).astype(BF16)
        ks_scr[crow(c), :] = (k_in * jnp.exp(b_last - b_mid)).astype(BF16)
        qi_scr[crow(c), :] = q_in.astype(BF16)
        ki_scr[crow(c), :] = k_in.astype(BF16)
        dc_scr[c] = jnp.broadcast_to(jnp.exp(b_last), (SUB, KEY))

    for c in range(n_chunks):
        for h in range(HEADS):
            kv_scr[c, h] = lax.dot_general(ks_scr[crow(c), ksl(h)], v_ref[crow(c), vsl(h)], tn,
                                           preferred_element_type=F32)
    for c in range(n_chunks):
        for h in range(HEADS):
            sc = lax.dot_general(qi_scr[crow(c), ksl(h)], ki_scr[crow(c), ksl(h)], nt,
                                 preferred_element_type=F32)
            sc_scr[c, h] = jnp.where(causal, sc, 0.0).astype(BF16)

    for c in range(n_chunks):
        for h in range(HEADS):
            s_old = s_scr[h]
            sb_scr[c, h] = s_old.astype(BF16)
            dcb = jnp.broadcast_to(dc_scr[c, 0:1, ksl(h)], (DK, DK)).T
            s_scr[h] = jnp.concatenate([dcb, dcb], axis=1) * s_old + kv_scr[c, h]

    for c in range(n_chunks):
        for h in range(HEADS):
            lhs = jnp.concatenate([qs_scr[crow(c), ksl(h)], sc_scr[c, h]], axis=1)
            rhs = jnp.concatenate([sb_scr[c, h], v_ref[crow(c), vsl(h)]], axis=0)
            o = _dot(lhs, rhs)
            ms = jnp.mean(o * o, axis=-1, keepdims=True)
            og = o * lax.rsqrt(ms + EPS) * gn_ref[:, vsl(h)] * r_ref[crow(c), vsl(h)].astype(F32)
            o_ref[crow(c), vsl(h)] = og.astype(BF16)

    @pl.when(j == pl.num_programs(1) - 1)
    def _():
        sf_ref[...] = s_scr[...]


def _gla(q, k, v, la, r, gn, s0, *, nb, rows_per_b, chunks_per_step):
    nc = chunks_per_step
    tm = nc * CHUNK
    nt = rows_per_b // tm
    row_spec = lambda w: pl.BlockSpec((tm, w), lambda b, i: (b * nt + i, 0))
    state_spec = pl.BlockSpec((None, HEADS, DK, DV), lambda b, i: (b, 0, 0, 0))
    return pl.pallas_call(
        functools.partial(_gla_kernel, n_chunks=nc),
        grid=(nb, nt),
        in_specs=[row_spec(KEY), row_spec(KEY), row_spec(VAL), row_spec(KEY), row_spec(VAL),
                  _const_spec(gn.shape), state_spec],
        out_specs=[row_spec(VAL), state_spec],
        out_shape=[jax.ShapeDtypeStruct((nb * rows_per_b, VAL), BF16),
                   jax.ShapeDtypeStruct((nb, HEADS, DK, DV), F32)],
        scratch_shapes=[pltpu.VMEM((HEADS, DK, DV), F32),
                        pltpu.VMEM((tm, KEY), F32),
                        pltpu.VMEM((tm, KEY), BF16), pltpu.VMEM((tm, KEY), BF16),
                        pltpu.VMEM((tm, KEY), BF16), pltpu.VMEM((tm, KEY), BF16),
                        pltpu.VMEM((nc, SUB, KEY), F32),
                        pltpu.VMEM((nc, HEADS, DK, DV), F32),
                        pltpu.VMEM((nc, HEADS, CHUNK, CHUNK), BF16),
                        pltpu.VMEM((nc, HEADS, DK, DV), BF16)],
        name="gla",
        compiler_params=pltpu.CompilerParams(
            dimension_semantics=("parallel", "arbitrary"), vmem_limit_bytes=VMEM_LIMIT),
    )(q, k, v, la, r, gn, s0)


def _s5disc_kernel(ar_ref, ai_ref, ls_ref, br_ref, bi_ref, lre_o, lim_o, bbre_o, bbim_o):
    ar = ar_ref[...]
    ai = ai_ref[...]
    dt = jnp.exp(ls_ref[...])
    mag = jnp.exp(ar * dt)
    lre = mag * jnp.cos(ai * dt)
    lim = mag * jnp.sin(ai * dt)
    zr = lre - 1.0
    zi = lim
    den = ar * ar + ai * ai
    fr = (zr * ar + zi * ai) / den
    fi = (zi * ar - zr * ai) / den
    br = br_ref[...]
    bi = bi_ref[...]
    lre_o[...] = lre
    lim_o[...] = lim
    bbre_o[...] = fr * br - fi * bi
    bbim_o[...] = fr * bi + fi * br


def _s5disc(ar, ai, ls, br, bi):
    shp = jax.ShapeDtypeStruct((WIDTH, NSTATE), F32)
    return pl.pallas_call(
        _s5disc_kernel, out_shape=[shp, shp, shp, shp], name="s5disc",
    )(ar, ai, ls, br, bi)


S5_T = 64
S5_PITCH = S5_T + 4
N_TILE = SLAB_STATE // LANE
TILE2 = 2 * LANE
SCAN_GROUP = 4


def _s5_kernel(u_ref, bblk_ref, cblk_ref, lre_ref, lim_ref, d_ref, h0_ref, y_ref, hf_ref,
               stg_in, lhs_scr, bu_scr, s_scr, acc_scr, stg_out, st_scr):
    i = pl.program_id(0)

    @pl.when(i == 0)
    def _():
        st_scr[...] = h0_ref[...]

    T, P = S5_T, S5_PITCH
    lane_tile = lambda lt: slice(lt * LANE, (lt + 1) * LANE)

    for lt in range(WIDTH // LANE):
        for b in range(SUB):
            stg_in[lt, b * P:b * P + T, :] = u_ref[b, :, lane_tile(lt)].astype(F32)
    for lt in range(WIDTH // LANE):
        for tp in range(T // 2):
            x0 = stg_in[lt, pl.ds(2 * tp, SUB, stride=P), :]
            x1 = stg_in[lt, pl.ds(2 * tp + 1, SUB, stride=P), :]
            lhs_scr[2 * SUB * tp:2 * SUB * (tp + 1), lane_tile(lt)] = (
                jnp.concatenate([x0, x1], axis=0).astype(BF16))

    def in_piece(q, j):
        cols = slice(j * TILE2, (j + 1) * TILE2)
        bu_scr[q % 2, :, cols] = _dot(lhs_scr[:, q * SLAB:(q + 1) * SLAB], bblk_ref[q, :, cols])

    def out_piece(q, j):
        cols = slice(j * TILE2, (j + 1) * TILE2)
        part = _dot(s_scr[q % 2, :, cols], cblk_ref[q, cols, :])
        if j == 0:
            acc_scr[q % 2] = part
        else:
            acc_scr[q % 2] += part

    def scan_steps(q, lr, li, state, t0):
        for t in range(t0, t0 + SCAN_GROUP, 2):
            for j in range(N_TILE):
                re_c = slice(j * TILE2, j * TILE2 + LANE)
                im_c = slice(j * TILE2 + LANE, (j + 1) * TILE2)
                s_re, s_im = state[j]
                o_re, o_im = [], []
                for dt in range(2):
                    rows = slice(SUB * (t + dt), SUB * (t + dt + 1))
                    n_re = lr[j] * s_re - li[j] * s_im + bu_scr[q % 2, rows, re_c]
                    n_im = lr[j] * s_im + li[j] * s_re + bu_scr[q % 2, rows, im_c]
                    s_re, s_im = n_re, n_im
                    o_re.append(n_re)
                    o_im.append(n_im)
                state[j] = (s_re, s_im)
                rows2 = slice(SUB * t, SUB * (t + 2))
                s_scr[q % 2, rows2, re_c] = jnp.concatenate(o_re, axis=0).astype(BF16)
                s_scr[q % 2, rows2, im_c] = jnp.concatenate(o_im, axis=0).astype(BF16)

    def regroup_out(q):
        for l2 in range(SLAB // LANE):
            lt = q * (SLAB // LANE) + l2
            for t in range(T):
                stg_out[lt, pl.ds(t, SUB, stride=P), :] = (
                    acc_scr[q % 2, SUB * t:SUB * (t + 1), lane_tile(l2)])
            for b in range(SUB):
                y = (stg_out[lt, b * P:b * P + T, :]
                     + d_ref[:, lane_tile(lt)] * u_ref[b, :, lane_tile(lt)].astype(F32))
                y_ref[b, :, lane_tile(lt)] = y.astype(BF16)

    n_groups = T // SCAN_GROUP
    for stage in range(N_SLAB + 2):
        q_in, q_scan, q_out = stage, stage - 1, stage - 2
        pieces = []
        for j in range(N_TILE):
            if q_in < N_SLAB:
                pieces.append(functools.partial(in_piece, q_in, j))
            if 0 <= q_out < N_SLAB:
                pieces.append(functools.partial(out_piece, q_out, j))
        scanning = 0 <= q_scan < N_SLAB
        if scanning:
            lr = [jnp.broadcast_to(lre_ref[q_scan:q_scan + 1, lane_tile(j)], (SUB, LANE))
                  for j in range(N_TILE)]
            li = [jnp.broadcast_to(lim_ref[q_scan:q_scan + 1, lane_tile(j)], (SUB, LANE))
                  for j in range(N_TILE)]
            state = [(st_scr[q_scan, :, j * TILE2:j * TILE2 + LANE],
                      st_scr[q_scan, :, j * TILE2 + LANE:(j + 1) * TILE2]) for j in range(N_TILE)]
        for g in range(n_groups):
            for p in pieces[g * len(pieces) // n_groups:(g + 1) * len(pieces) // n_groups]:
                p()
            if scanning:
                scan_steps(q_scan, lr, li, state, g * SCAN_GROUP)
        if scanning:
            for j in range(N_TILE):
                st_scr[q_scan, :, j * TILE2:j * TILE2 + LANE] = state[j][0]
                st_scr[q_scan, :, j * TILE2 + LANE:(j + 1) * TILE2] = state[j][1]
        if 0 <= q_out < N_SLAB:
            regroup_out(q_out)

    @pl.when(i == pl.num_programs(0) - 1)
    def _():
        hf_ref[...] = st_scr[...]


def _s5(u3, bblk, cblk, lre, lim, d, h0):
    nb, t_total, _ = u3.shape
    assert nb == SUB and t_total % S5_T == 0
    blk = pl.BlockSpec((SUB, S5_T, WIDTH), lambda i: (0, i, 0))
    rows = SUB * S5_T
    n_lt = WIDTH // LANE
    return pl.pallas_call(
        _s5_kernel,
        grid=(t_total // S5_T,),
        in_specs=[blk, _const_spec(bblk.shape), _const_spec(cblk.shape),
                  _const_spec(lre.shape), _const_spec(lim.shape), _const_spec(d.shape),
                  _const_spec(h0.shape)],
        out_specs=[blk, _const_spec(h0.shape)],
        out_shape=[jax.ShapeDtypeStruct(u3.shape, BF16), jax.ShapeDtypeStruct(h0.shape, F32)],
        scratch_shapes=[pltpu.VMEM((n_lt, SUB * S5_PITCH, LANE), F32),
                        pltpu.VMEM((rows, WIDTH), BF16),
                        pltpu.VMEM((2, rows, 2 * SLAB_STATE), F32),
                        pltpu.VMEM((2, rows, 2 * SLAB_STATE), BF16),
                        pltpu.VMEM((2, rows, SLAB), F32),
                        pltpu.VMEM((n_lt, SUB * S5_PITCH, LANE), F32),
                        pltpu.VMEM(h0.shape, F32)],
        name="s5",
        compiler_params=pltpu.CompilerParams(
            dimension_semantics=("arbitrary",), vmem_limit_bytes=VMEM_LIMIT),
    )(u3, bblk, cblk, lre, lim, d, h0)


FF_SPLIT = 4


def _post_kernel(x_ref, og_ref, ys_ref, sg_ref, ss_ref, wo, wglu, bglu, wout, gpost, gffn,
                 w1, w2, gfpost, out_ref):
    y_gla = _dot(og_ref[...], wo[...])
    ys = ys_ref[...].astype(F32)
    gs = 0.5 * ys * (1.0 + jnp.tanh(0.7978845608028654 * (ys + 0.044715 * (ys * ys * ys))))
    glu = _dot(gs.astype(BF16), wglu[...]) + bglu[...]
    y_ssm = glu[:, :D_MODEL] * _sigmoid(glu[:, D_MODEL:])
    mixed = sg_ref[...].astype(F32) * y_gla + ss_ref[...].astype(F32) * y_ssm
    h1 = x_ref[...] + _rms(_dot(mixed.astype(BF16), wout[...]), gpost[...])
    hn = _rms(h1, gffn[...]).astype(BF16)
    wcols = D_FF // FF_SPLIT
    f = None
    for c in range(FF_SPLIT):
        a = jnp.maximum(_dot(hn, w1[:, c * wcols:(c + 1) * wcols]), 0.0)
        part = _dot((a * a).astype(BF16), w2[c * wcols:(c + 1) * wcols, :])
        f = part if f is None else f + part
    out_ref[...] = h1 + _rms(f, gfpost[...])


def _post(x2, og, ys, sg, ss, consts, *, nb, rows_per_b, tm):
    nt = rows_per_b // tm
    row_spec = lambda w: pl.BlockSpec((tm, w), lambda b, i: (b * nt + i, 0))
    in_specs = [row_spec(D_MODEL), row_spec(VAL), row_spec(WIDTH), row_spec(D_MODEL), row_spec(D_MODEL)]
    in_specs += [_const_spec(c.shape) for c in consts]
    return pl.pallas_call(
        _post_kernel, grid=(nb, nt), in_specs=in_specs, out_specs=row_spec(D_MODEL),
        out_shape=jax.ShapeDtypeStruct((nb * rows_per_b, D_MODEL), F32), name="post",
        compiler_params=pltpu.CompilerParams(
            dimension_semantics=("parallel", "parallel"), vmem_limit_bytes=VMEM_LIMIT),
    )(x2, og, ys, sg, ss, *consts)


def _block_diag(m):
    eye = jnp.eye(SLAB_GROUPS, dtype=m.dtype)
    out = jnp.einsum("qgab,gk->qgakb", m, eye)
    return out.reshape(N_SLAB, SLAB_GROUPS * m.shape[2], SLAB_GROUPS * m.shape[3])


def kernel(x, meta_tokens, g_mix_pre, w_in, w_gate_up, b_gate, gla_norm_g, w_o_gla, a_re, a_im,
           log_step, b_re, b_im, c_re, c_im, d_skip, w_glu, b_glu, w_out, g_mix_post, g_ffn_pre,
           w_ff1, w_ff2, g_ffn_post):
    bsz, seq, dm = x.shape
    assert dm == D_MODEL and bsz == SUB and seq % CHUNK == 0
    assert w_in.shape[0] == 1, "single layer"
    l = 0
    row = lambda t: t.reshape(1, -1).astype(F32)

    w = w_in[l]
    off = [0, KEY, 2 * KEY, 2 * KEY + VAL, 2 * KEY + 2 * VAL]
    wq = w[:, off[0]:off[1]].astype(BF16)
    wk = w[:, off[1]:off[2]].astype(BF16)
    wv = w[:, off[2]:off[3]].astype(BF16)
    wr = w[:, off[3]:off[4]].astype(BF16)
    a0 = off[4]
    wa = jnp.pad(w[:, a0:a0 + RANK], ((0, 0), (0, LANE - RANK))).astype(BF16)
    u0 = a0 + RANK
    wu = w[:, u0:u0 + WIDTH].astype(BF16)
    wzg = w[:, u0 + WIDTH:u0 + WIDTH + D_MODEL].astype(BF16)
    wzs = w[:, u0 + WIDTH + D_MODEL:].astype(BF16)
    ws = (wq, wk, wv, wr, wa, wu, wzg, wzs)
    wup = jnp.pad(w_gate_up[l], ((0, LANE - RANK), (0, 0))).astype(BF16)
    bg = row(b_gate[l])
    gpre = row(g_mix_pre[l])
    gn = row(gla_norm_g[l])

    rep = lambda t: jnp.repeat(t.astype(F32), GSIZE, axis=0)
    br_t = jnp.swapaxes(b_re[l], 1, 2).reshape(WIDTH, NSTATE).astype(F32)
    bi_t = jnp.swapaxes(b_im[l], 1, 2).reshape(WIDTH, NSTATE).astype(F32)
    lre_w, lim_w, bbre, bbim = _s5disc(rep(a_re[l]), rep(a_im[l]), rep(log_step[l][:, None]),
                                       br_t, bi_t)
    lam_re = lre_w[::GSIZE].reshape(N_SLAB, SLAB_STATE)
    lam_im = lim_w[::GSIZE].reshape(N_SLAB, SLAB_STATE)
    grp4 = lambda t, a, b: t.reshape(N_SLAB, SLAB_GROUPS, a, b)
    bb_re = _block_diag(grp4(bbre, GSIZE, NSTATE)).reshape(N_SLAB, SLAB, N_TILE, 1, LANE)
    bb_im = _block_diag(grp4(bbim, GSIZE, NSTATE)).reshape(N_SLAB, SLAB, N_TILE, 1, LANE)
    bblk = jnp.concatenate([bb_re, bb_im], axis=3).reshape(N_SLAB, SLAB, 2 * SLAB_STATE).astype(BF16)
    cr_t = jnp.swapaxes(c_re[l].astype(F32), 1, 2)
    ci_t = jnp.swapaxes(c_im[l].astype(F32), 1, 2)
    cc_re = _block_diag(grp4(cr_t, NSTATE, GSIZE)).reshape(N_SLAB, N_TILE, 1, LANE, SLAB)
    cc_im = _block_diag(grp4(-ci_t, NSTATE, GSIZE)).reshape(N_SLAB, N_TILE, 1, LANE, SLAB)
    cblk = jnp.concatenate([cc_re, cc_im], axis=2).reshape(N_SLAB, 2 * SLAB_STATE, SLAB).astype(BF16)
    dsk = row(d_skip[l])

    mq, mk, mv, mr, mla, mu, _, _ = _inproj(
        meta_tokens.astype(F32), gpre, ws, wup, bg, nb=1, rows_per_b=N_META, tm=N_META)
    padr = lambda t: jnp.pad(t, ((N_PAD, 0), (0, 0)))
    s_zero = jnp.zeros((1, HEADS, DK, DV), F32)
    _, s_meta = _gla(padr(mq), padr(mk), padr(mv), padr(mla), padr(mr), gn, s_zero,
                     nb=1, rows_per_b=CHUNK, chunks_per_step=1)
    mu3 = jnp.broadcast_to(padr(mu)[None], (SUB, CHUNK, WIDTH))
    h_zero = jnp.zeros((N_SLAB, SUB, 2 * SLAB_STATE), F32)
    _, h_meta = _s5(mu3, bblk, cblk, lam_re, lam_im, dsk, h_zero)

    x2 = x.reshape(bsz * seq, dm)
    tm = 512 if seq % 512 == 0 else CHUNK
    q, k, v, r, la, u, sg, ss = _inproj(x2, gpre, ws, wup, bg, nb=bsz, rows_per_b=seq, tm=tm)
    s0 = jnp.broadcast_to(s_meta, (bsz, HEADS, DK, DV))
    og, _ = _gla(q, k, v, la, r, gn, s0, nb=bsz, rows_per_b=seq, chunks_per_step=tm // CHUNK)
    ys, _ = _s5(u.reshape(bsz, seq, WIDTH), bblk, cblk, lam_re, lam_im, dsk, h_meta)
    consts = (w_o_gla[l].astype(BF16), w_glu[l].astype(BF16), row(b_glu[l]), w_out[l].astype(BF16),
              row(g_mix_post[l]), row(g_ffn_pre[l]), w_ff1[l].astype(BF16), w_ff2[l].astype(BF16),
              row(g_ffn_post[l]))
    out = _post(x2, og, ys.reshape(bsz * seq, WIDTH), sg, ss, consts, nb=bsz, rows_per_b=seq, tm=tm)
    return out.reshape(bsz, seq, dm)
```

```python
import functools

import jax
import jax.numpy as jnp
from jax import lax
from jax.experimental import pallas as pl
from jax.experimental.pallas import tpu as pltpu

F32 = jnp.float32
BF16 = jnp.bfloat16

D_MODEL = 1024
N_META = 16
CHUNK = 64
N_PAD = CHUNK - N_META
HEADS = 4
DK = 128
DV = 256
KEY = HEADS * DK
VAL = HEADS * DV
RANK = 16
TAU = 16.0
GROUPS = 64
GSIZE = 16
NSTATE = 64
WIDTH = GROUPS * GSIZE
D_FF = 4 * D_MODEL
EPS = 1e-6

LANE = 128
SUB = 8
SLAB = 256
SLAB_GROUPS = SLAB // GSIZE
N_SLAB = WIDTH // SLAB
SLAB_STATE = SLAB_GROUPS * NSTATE
N_TILE = SLAB_STATE // LANE
TILE2 = 2 * LANE
VMEM_LIMIT = 56 * 1024 * 1024


def _const_spec(shape):
    nd = len(shape)
    return pl.BlockSpec(shape, lambda *_: (0,) * nd, pipeline_mode=pl.Buffered(1))


def _rms(x, g):
    ms = jnp.mean(x * x, axis=-1, keepdims=True)
    return x * lax.rsqrt(ms + EPS) * g


def _sigmoid(x):
    return 1.0 / (1.0 + jnp.exp(-x))


def _dot(a, b):
    return jnp.dot(a, b, preferred_element_type=F32)


def _inproj_kernel(x_ref, g_ref, wq, wk, wv, wr, wa, wu, wzg, wzs, wup, bg,
                   q_o, k_o, v_o, r_o, la_o, u_o, zg_o, zs_o):
    xn = _rms(x_ref[...], g_ref[...]).astype(BF16)

    def mm(w):
        return _dot(xn, w[...])

    q_o[...] = (mm(wq) * (DK ** -0.5)).astype(BF16)
    k_o[...] = mm(wk).astype(BF16)
    v_o[...] = mm(wv).astype(BF16)
    r = mm(wr)
    r_o[...] = (r * _sigmoid(r)).astype(BF16)
    z = _dot(mm(wa).astype(BF16), wup[...]) + bg[...]
    la_o[...] = (jnp.minimum(z, 0.0) - jnp.log(1.0 + jnp.exp(-jnp.abs(z)))) * (1.0 / TAU)
    u_o[...] = mm(wu).astype(BF16)
    zg_o[...] = _sigmoid(mm(wzg)).astype(BF16)
    zs_o[...] = _sigmoid(mm(wzs)).astype(BF16)


def _inproj(x2, g, ws, wup, bg, *, nb, rows_per_b, tm):
    rows = nb * rows_per_b
    nt = rows_per_b // tm
    row_spec = lambda w: pl.BlockSpec((tm, w), lambda b, i: (b * nt + i, 0))
    out_shape = [
        jax.ShapeDtypeStruct((rows, KEY), BF16), jax.ShapeDtypeStruct((rows, KEY), BF16),
        jax.ShapeDtypeStruct((rows, VAL), BF16), jax.ShapeDtypeStruct((rows, VAL), BF16),
        jax.ShapeDtypeStruct((rows, KEY), F32), jax.ShapeDtypeStruct((rows, WIDTH), BF16),
        jax.ShapeDtypeStruct((rows, D_MODEL), BF16), jax.ShapeDtypeStruct((rows, D_MODEL), BF16),
    ]
    out_specs = [row_spec(KEY), row_spec(KEY), row_spec(VAL), row_spec(VAL), row_spec(KEY),
                 row_spec(WIDTH), row_spec(D_MODEL), row_spec(D_MODEL)]
    in_specs = [row_spec(D_MODEL), _const_spec(g.shape)]
    in_specs += [_const_spec(w.shape) for w in ws]
    in_specs += [_const_spec(wup.shape), _const_spec(bg.shape)]
    return pl.pallas_call(
        _inproj_kernel, grid=(nb, nt), in_specs=in_specs, out_specs=out_specs,
        out_shape=out_shape, name="inproj",
        compiler_params=pltpu.CompilerParams(
            dimension_semantics=("parallel", "parallel"), vmem_limit_bytes=VMEM_LIMIT),
    )(x2, g, *ws, wup, bg)


def _gla_kernel(q_ref, k_ref, v_ref, la_ref, r_ref, gn_ref, s0_ref, o_ref, sf_ref,
                s_scr, b_scr, qi_scr, ki_scr, qs_scr, ks_scr, dc_scr, kv_scr, sc_scr, sb_scr,
                *, n_chunks):
    j = pl.program_id(1)

    @pl.when(j == 0)
    def _():
        s_scr[...] = s0_ref[...]

    row = lax.broadcasted_iota(jnp.int32, (CHUNK, CHUNK), 0)
    col = lax.broadcasted_iota(jnp.int32, (CHUNK, CHUNK), 1)
    causal = row >= col
    tril = jnp.where(causal, 1.0, 0.0).astype(BF16)
    tn = (((0,), (0,)), ((), ()))
    nt = (((1,), (1,)), ((), ()))

    crow = lambda c: slice(c * CHUNK, (c + 1) * CHUNK)
    ksl = lambda h: slice(h * DK, (h + 1) * DK)
    vsl = lambda h: slice(h * DV, (h + 1) * DV)

    for c in range(n_chunks):
        la = la_ref[crow(c), :]
        la_hi = la.astype(BF16)
        la_lo = (la - la_hi.astype(F32)).astype(BF16)
        b_scr[crow(c), :] = _dot(tril, la_hi) + _dot(tril, la_lo)

    for c in range(n_chunks):
        b = b_scr[crow(c), :]
        b_mid = b[CHUNK // 2:CHUNK // 2 + 1, :]
        b_last = b[CHUNK - 1:CHUNK, :]
        q_in = q_ref[crow(c), :].astype(F32) * jnp.exp(b - b_mid)
        k_in = k_ref[crow(c), :].astype(F32) * jnp.exp(b_mid - b)
        qs_scr[crow(c), :] = (q_in * jnp.exp(b_mid)).astype(BF16)
        ks_scr[crow(c), :] = (k_in * jnp.exp(b_last - b_mid)).astype(BF16)
        qi_scr[crow(c), :] = q_in.astype(BF16)
        ki_scr[crow(c), :] = k_in.astype(BF16)
        dc_scr[c] = jnp.broadcast_to(jnp.exp(b_last), (SUB, KEY))

    for c in range(n_chunks):
        for h in range(HEADS):
            kv_scr[c, h] = lax.dot_general(ks_scr[crow(c), ksl(h)], v_ref[crow(c), vsl(h)], tn,
                                           preferred_element_type=F32)
    for c in range(n_chunks):
        for h in range(HEADS):
            sc = lax.dot_general(qi_scr[crow(c), ksl(h)], ki_scr[crow(c), ksl(h)], nt,
                                 preferred_element_type=F32)
            sc_scr[c, h] = jnp.where(causal, sc, 0.0).astype(BF16)

    for c in range(n_chunks):
        for h in range(HEADS):
            s_old = s_scr[h]
            sb_scr[c, h] = s_old.astype(BF16)
            dcb = jnp.broadcast_to(dc_scr[c, 0:1, ksl(h)], (DK, DK)).T
            s_scr[h] = jnp.concatenate([dcb, dcb], axis=1) * s_old + kv_scr[c, h]

    for c in range(n_chunks):
        for h in range(HEADS):
            lhs = jnp.concatenate([qs_scr[crow(c), ksl(h)], sc_scr[c, h]], axis=1)
            rhs = jnp.concatenate([sb_scr[c, h], v_ref[crow(c), vsl(h)]], axis=0)
            o = _dot(lhs, rhs)
            ms = jnp.mean(o * o, axis=-1, keepdims=True)
            og = o * lax.rsqrt(ms + EPS) * gn_ref[:, vsl(h)] * r_ref[crow(c), vsl(h)].astype(F32)
            o_ref[crow(c), vsl(h)] = og.astype(BF16)

    @pl.when(j == pl.num_programs(1) - 1)
    def _():
        sf_ref[...] = s_scr[...]


def _gla(q, k, v, la, r, gn, s0, *, nb, rows_per_b, chunks_per_step):
    nc = chunks_per_step
    tm = nc * CHUNK
    nt = rows_per_b // tm
    row_spec = lambda w: pl.BlockSpec((tm, w), lambda b, i: (b * nt + i, 0))
    state_spec = pl.BlockSpec((None, HEADS, DK, DV), lambda b, i: (b, 0, 0, 0))
    s0_spec = pl.BlockSpec((None, HEADS, DK, DV), lambda b, i: (0, 0, 0, 0))
    assert s0.shape == (1, HEADS, DK, DV)
    return pl.pallas_call(
        functools.partial(_gla_kernel, n_chunks=nc),
        grid=(nb, nt),
        in_specs=[row_spec(KEY), row_spec(KEY), row_spec(VAL), row_spec(KEY), row_spec(VAL),
                  _const_spec(gn.shape), s0_spec],
        out_specs=[row_spec(VAL), state_spec],
        out_shape=[jax.ShapeDtypeStruct((nb * rows_per_b, VAL), BF16),
                   jax.ShapeDtypeStruct((nb, HEADS, DK, DV), F32)],
        scratch_shapes=[pltpu.VMEM((HEADS, DK, DV), F32),
                        pltpu.VMEM((tm, KEY), F32),
                        pltpu.VMEM((tm, KEY), BF16), pltpu.VMEM((tm, KEY), BF16),
                        pltpu.VMEM((tm, KEY), BF16), pltpu.VMEM((tm, KEY), BF16),
                        pltpu.VMEM((nc, SUB, KEY), F32),
                        pltpu.VMEM((nc, HEADS, DK, DV), F32),
                        pltpu.VMEM((nc, HEADS, CHUNK, CHUNK), BF16),
                        pltpu.VMEM((nc, HEADS, DK, DV), BF16)],
        name="gla",
        compiler_params=pltpu.CompilerParams(
            dimension_semantics=("parallel", "arbitrary"), vmem_limit_bytes=VMEM_LIMIT),
    )(q, k, v, la, r, gn, s0)


def _lambda_bar(ar, ai, ls):
    dt = jnp.exp(ls)
    mag = jnp.exp(ar * dt)
    return mag * jnp.cos(ai * dt), mag * jnp.sin(ai * dt)


def _s5disc_kernel(ars_ref, ais_ref, lss_ref, ar_ref, ai_ref, ls_ref, br_ref, bi_ref, cr_ref, ci_ref,
                   lre_o, lim_o, bblk_o, cblk_o):
    lre_s, lim_s = _lambda_bar(ars_ref[...], ais_ref[...], lss_ref[...])
    lre_o[...] = lre_s
    lim_o[...] = lim_s

    ar = ar_ref[...]
    ai = ai_ref[...]
    lre, lim = _lambda_bar(ar, ai, ls_ref[...])
    zr = lre - 1.0
    zi = lim
    den = ar * ar + ai * ai
    fr = (zr * ar + zi * ai) / den
    fi = (zi * ar - zr * ai) / den
    br = br_ref[...]
    bi = bi_ref[...]
    bb = (fr * br - fi * bi, fr * bi + fi * br)

    grp_of_row = lax.broadcasted_iota(jnp.int32, (SLAB, LANE), 0) // GSIZE
    half_of_lane = lax.broadcasted_iota(jnp.int32, (SLAB, LANE), 1) // NSTATE
    grp_of_col = lax.broadcasted_iota(jnp.int32, (LANE, SLAB), 1) // GSIZE
    half_of_row = lax.broadcasted_iota(jnp.int32, (LANE, SLAB), 0) // NSTATE
    for q in range(N_SLAB):
        ch = slice(q * SLAB, (q + 1) * SLAB)
        b_two = [jnp.concatenate([p[ch, :], p[ch, :]], axis=1) for p in bb]
        c_two = [jnp.concatenate([p[:, ch], p[:, ch]], axis=0)
                 for p in (cr_ref, ci_ref)]
        c_two[1] = -c_two[1]
        for j in range(N_TILE):
            own_b = grp_of_row == 2 * j + half_of_lane
            own_c = grp_of_col == 2 * j + half_of_row
            for p in range(2):
                cols = slice(j * TILE2 + p * LANE, j * TILE2 + (p + 1) * LANE)
                bblk_o[q, :, cols] = jnp.where(own_b, b_two[p], 0.0).astype(BF16)
                cblk_o[q, cols, :] = jnp.where(own_c, c_two[p], 0.0).astype(BF16)


def _s5disc(ars, ais, lss, ar, ai, ls, br, bi, cr, ci):
    lam = jax.ShapeDtypeStruct((N_SLAB, SLAB_STATE), F32)
    return pl.pallas_call(
        _s5disc_kernel,
        out_shape=[lam, lam, jax.ShapeDtypeStruct((N_SLAB, SLAB, 2 * SLAB_STATE), BF16),
                   jax.ShapeDtypeStruct((N_SLAB, 2 * SLAB_STATE, SLAB), BF16)],
        name="s5disc",
    )(ars, ais, lss, ar, ai, ls, br, bi, cr, ci)


S5_T = 64
S5_PITCH = S5_T + 4
SCAN_TILES = 4
SCAN_GROUP = 8


def _s5_kernel(u_ref, bblk_ref, cblk_ref, lre_ref, lim_ref, d_ref, h0_ref, y_ref, hf_ref,
               stg_in, lhs_scr, bu_scr, s_scr, acc_scr, stg_out, st_scr):
    i = pl.program_id(0)

    @pl.when(i == 0)
    def _():
        st_scr[...] = h0_ref[...]

    T, P = S5_T, S5_PITCH
    lane_tile = lambda lt: slice(lt * LANE, (lt + 1) * LANE)

    def regroup_in(lt):
        for b in range(SUB):
            stg_in[lt, b * P:b * P + T, :] = u_ref[b, :, lane_tile(lt)].astype(F32)
        for tp in range(T // 2):
            x0 = stg_in[lt, pl.ds(2 * tp, SUB, stride=P), :]
            x1 = stg_in[lt, pl.ds(2 * tp + 1, SUB, stride=P), :]
            lhs_scr[2 * SUB * tp:2 * SUB * (tp + 1), lane_tile(lt)] = (
                jnp.concatenate([x0, x1], axis=0).astype(BF16))

    def in_piece(q, j):
        cols = slice(j * TILE2, (j + 1) * TILE2)
        bu_scr[q % 2, :, cols] = _dot(lhs_scr[:, q * SLAB:(q + 1) * SLAB], bblk_ref[q, :, cols])

    def out_piece(q, j):
        cols = slice(j * TILE2, (j + 1) * TILE2)
        part = _dot(s_scr[q % 2, :, cols], cblk_ref[q, cols, :])
        if j == 0:
            acc_scr[q % 2] = part
        else:
            acc_scr[q % 2] += part

    def scan_steps(q, tiles, lr, li, state, t0):
        for t in range(t0, t0 + SCAN_GROUP, 2):
            for j in tiles:
                re_c = slice(j * TILE2, j * TILE2 + LANE)
                im_c = slice(j * TILE2 + LANE, (j + 1) * TILE2)
                s_re, s_im = state[j]
                o_re, o_im = [], []
                for dt in range(2):
                    rows = slice(SUB * (t + dt), SUB * (t + dt + 1))
                    n_re = lr[j] * s_re - li[j] * s_im + bu_scr[q % 2, rows, re_c]
                    n_im = lr[j] * s_im + li[j] * s_re + bu_scr[q % 2, rows, im_c]
                    s_re, s_im = n_re, n_im
                    o_re.append(n_re)
                    o_im.append(n_im)
                state[j] = (s_re, s_im)
                rows2 = slice(SUB * t, SUB * (t + 2))
                s_scr[q % 2, rows2, re_c] = jnp.concatenate(o_re, axis=0).astype(BF16)
                s_scr[q % 2, rows2, im_c] = jnp.concatenate(o_im, axis=0).astype(BF16)

    def regroup_out(q):
        for l2 in range(SLAB // LANE):
            lt = q * (SLAB // LANE) + l2
            for t in range(T):
                stg_out[lt, pl.ds(t, SUB, stride=P), :] = (
                    acc_scr[q % 2, SUB * t:SUB * (t + 1), lane_tile(l2)])
            for b in range(SUB):
                y = (stg_out[lt, b * P:b * P + T, :]
                     + d_ref[:, lane_tile(lt)] * u_ref[b, :, lane_tile(lt)].astype(F32))
                y_ref[b, :, lane_tile(lt)] = y.astype(BF16)

    groups_per_run = T // SCAN_GROUP
    n_groups = groups_per_run * (N_TILE // SCAN_TILES)
    lt_per_slab = SLAB // LANE
    for lt in range(lt_per_slab):
        regroup_in(lt)
    for stage in range(N_SLAB + 2):
        q_in, q_scan, q_out = stage, stage - 1, stage - 2
        pieces = []
        for j in range(N_TILE):
            if q_in < N_SLAB:
                pieces.append(functools.partial(in_piece, q_in, j))
            if 0 <= q_out < N_SLAB:
                pieces.append(functools.partial(out_piece, q_out, j))
            if stage == 0 and j < WIDTH // LANE - lt_per_slab:
                pieces.append(functools.partial(regroup_in, lt_per_slab + j))
        scanning = 0 <= q_scan < N_SLAB
        for g in range(n_groups):
            for p in pieces[g * len(pieces) // n_groups:(g + 1) * len(pieces) // n_groups]:
                p()
            if not scanning:
                continue
            run, g_in_run = divmod(g, groups_per_run)
            tiles = range(run * SCAN_TILES, (run + 1) * SCAN_TILES)
            if g_in_run == 0:
                lr = {j: jnp.broadcast_to(lre_ref[q_scan:q_scan + 1, lane_tile(j)], (SUB, LANE))
                      for j in tiles}
                li = {j: jnp.broadcast_to(lim_ref[q_scan:q_scan + 1, lane_tile(j)], (SUB, LANE))
                      for j in tiles}
                state = {j: (st_scr[q_scan, :, j * TILE2:j * TILE2 + LANE],
                             st_scr[q_scan, :, j * TILE2 + LANE:(j + 1) * TILE2]) for j in tiles}
            scan_steps(q_scan, tiles, lr, li, state, g_in_run * SCAN_GROUP)
            if g_in_run == groups_per_run - 1:
                for j in tiles:
                    st_scr[q_scan, :, j * TILE2:j * TILE2 + LANE] = state[j][0]
                    st_scr[q_scan, :, j * TILE2 + LANE:(j + 1) * TILE2] = state[j][1]
        if 0 <= q_out < N_SLAB:
            regroup_out(q_out)

    @pl.when(i == pl.num_programs(0) - 1)
    def _():
        hf_ref[...] = st_scr[...]


def _s5(u3, bblk, cblk, lre, lim, d, h0):
    nb, t_total, _ = u3.shape
    assert nb == SUB and t_total % S5_T == 0
    blk = pl.BlockSpec((SUB, S5_T, WIDTH), lambda i: (0, i, 0))
    rows = SUB * S5_T
    n_lt = WIDTH // LANE
    return pl.pallas_call(
        _s5_kernel,
        grid=(t_total // S5_T,),
        in_specs=[blk, _const_spec(bblk.shape), _const_spec(cblk.shape),
                  _const_spec(lre.shape), _const_spec(lim.shape), _const_spec(d.shape),
                  _const_spec(h0.shape)],
        out_specs=[blk, _const_spec(h0.shape)],
        out_shape=[jax.ShapeDtypeStruct(u3.shape, BF16), jax.ShapeDtypeStruct(h0.shape, F32)],
        scratch_shapes=[pltpu.VMEM((n_lt, SUB * S5_PITCH, LANE), F32),
                        pltpu.VMEM((rows, WIDTH), BF16),
                        pltpu.VMEM((2, rows, 2 * SLAB_STATE), F32),
                        pltpu.VMEM((2, rows, 2 * SLAB_STATE), BF16),
                        pltpu.VMEM((2, rows, SLAB), F32),
                        pltpu.VMEM((n_lt, SUB * S5_PITCH, LANE), F32),
                        pltpu.VMEM(h0.shape, F32)],
        name="s5",
        compiler_params=pltpu.CompilerParams(
            dimension_semantics=("arbitrary",), vmem_limit_bytes=VMEM_LIMIT),
    )(u3, bblk, cblk, lre, lim, d, h0)


FF_SPLIT = 4


def _post_kernel(x_ref, og_ref, ys_ref, sg_ref, ss_ref, wo, wglu, bglu, wout, gpost, gffn,
                 w1, w2, gfpost, out_ref):
    y_gla = _dot(og_ref[...], wo[...])
    ys = ys_ref[...].astype(F32)
    gs = 0.5 * ys * (1.0 + jnp.tanh(0.7978845608028654 * (ys + 0.044715 * (ys * ys * ys))))
    glu = _dot(gs.astype(BF16), wglu[...]) + bglu[...]
    y_ssm = glu[:, :D_MODEL] * _sigmoid(glu[:, D_MODEL:])
    mixed = sg_ref[...].astype(F32) * y_gla + ss_ref[...].astype(F32) * y_ssm
    h1 = x_ref[...] + _rms(_dot(mixed.astype(BF16), wout[...]), gpost[...])
    hn = _rms(h1, gffn[...]).astype(BF16)
    wcols = D_FF // FF_SPLIT
    f = None
    for c in range(FF_SPLIT):
        a = jnp.maximum(_dot(hn, w1[:, c * wcols:(c + 1) * wcols]), 0.0)
        part = _dot((a * a).astype(BF16), w2[c * wcols:(c + 1) * wcols, :])
        f = part if f is None else f + part
    out_ref[...] = h1 + _rms(f, gfpost[...])


def _post(x2, og, ys, sg, ss, consts, *, nb, rows_per_b, tm):
    nt = rows_per_b // tm
    row_spec = lambda w: pl.BlockSpec((tm, w), lambda b, i: (b * nt + i, 0))
    in_specs = [row_spec(D_MODEL), row_spec(VAL), row_spec(WIDTH), row_spec(D_MODEL), row_spec(D_MODEL)]
    in_specs += [_const_spec(c.shape) for c in consts]
    return pl.pallas_call(
        _post_kernel, grid=(nb, nt), in_specs=in_specs, out_specs=row_spec(D_MODEL),
        out_shape=jax.ShapeDtypeStruct((nb * rows_per_b, D_MODEL), F32), name="post",
        compiler_params=pltpu.CompilerParams(
            dimension_semantics=("parallel", "parallel"), vmem_limit_bytes=VMEM_LIMIT),
    )(x2, og, ys, sg, ss, *consts)


def kernel(x, meta_tokens, g_mix_pre, w_in, w_gate_up, b_gate, gla_norm_g, w_o_gla, a_re, a_im,
           log_step, b_re, b_im, c_re, c_im, d_skip, w_glu, b_glu, w_out, g_mix_post, g_ffn_pre,
           w_ff1, w_ff2, g_ffn_post):
    bsz, seq, dm = x.shape
    assert dm == D_MODEL and bsz == SUB and seq % CHUNK == 0
    assert w_in.shape[0] == 1, "single layer"
    l = 0
    row = lambda t: t.reshape(1, -1).astype(F32)

    w = w_in[l]
    off = [0, KEY, 2 * KEY, 2 * KEY + VAL, 2 * KEY + 2 * VAL]
    wq = w[:, off[0]:off[1]].astype(BF16)
    wk = w[:, off[1]:off[2]].astype(BF16)
    wv = w[:, off[2]:off[3]].astype(BF16)
    wr = w[:, off[3]:off[4]].astype(BF16)
    a0 = off[4]
    wa = jnp.pad(w[:, a0:a0 + RANK], ((0, 0), (0, LANE - RANK))).astype(BF16)
    u0 = a0 + RANK
    wu = w[:, u0:u0 + WIDTH].astype(BF16)
    wzg = w[:, u0 + WIDTH:u0 + WIDTH + D_MODEL].astype(BF16)
    wzs = w[:, u0 + WIDTH + D_MODEL:].astype(BF16)
    ws = (wq, wk, wv, wr, wa, wu, wzg, wzs)
    wup = jnp.pad(w_gate_up[l], ((0, LANE - RANK), (0, 0))).astype(BF16)
    bg = row(b_gate[l])
    gpre = row(g_mix_pre[l])
    gn = row(gla_norm_g[l])

    rep = lambda t: jnp.repeat(t.astype(F32), GSIZE, axis=0)
    slabs = lambda t: t.astype(F32).reshape(N_SLAB, SLAB_STATE)
    rows_gh = lambda t: jnp.swapaxes(t, 1, 2).reshape(WIDTH, NSTATE).astype(F32)
    cols_gh = lambda t: jnp.transpose(t, (2, 0, 1)).reshape(NSTATE, WIDTH).astype(F32)
    ls_gn = jnp.broadcast_to(log_step[l][:, None], (GROUPS, NSTATE))
    lam_re, lam_im, bblk, cblk = _s5disc(
        slabs(a_re[l]), slabs(a_im[l]), slabs(ls_gn), rep(a_re[l]), rep(a_im[l]), rep(ls_gn),
        rows_gh(b_re[l]), rows_gh(b_im[l]), cols_gh(c_re[l]), cols_gh(c_im[l]))
    dsk = row(d_skip[l])

    mq, mk, mv, mr, mla, mu, _, _ = _inproj(
        meta_tokens.astype(F32), gpre, ws, wup, bg, nb=1, rows_per_b=N_META, tm=N_META)
    padr = lambda t: jnp.pad(t, ((N_PAD, 0), (0, 0)))
    s_zero = jnp.zeros((1, HEADS, DK, DV), F32)
    _, s_meta = _gla(padr(mq), padr(mk), padr(mv), padr(mla), padr(mr), gn, s_zero,
                     nb=1, rows_per_b=CHUNK, chunks_per_step=1)
    mu3 = jnp.broadcast_to(padr(mu)[None], (SUB, CHUNK, WIDTH))
    h_zero = jnp.zeros((N_SLAB, SUB, 2 * SLAB_STATE), F32)
    _, h_meta = _s5(mu3, bblk, cblk, lam_re, lam_im, dsk, h_zero)

    x2 = x.reshape(bsz * seq, dm)
    tm = 512 if seq % 512 == 0 else CHUNK
    q, k, v, r, la, u, sg, ss = _inproj(x2, gpre, ws, wup, bg, nb=bsz, rows_per_b=seq, tm=tm)
    og, _ = _gla(q, k, v, la, r, gn, s_meta, nb=bsz, rows_per_b=seq, chunks_per_step=tm // CHUNK)
    ys, _ = _s5(u.reshape(bsz, seq, WIDTH), bblk, cblk, lam_re, lam_im, dsk, h_meta)
    consts = (w_o_gla[l].astype(BF16), w_glu[l].astype(BF16), row(b_glu[l]), w_out[l].astype(BF16),
              row(g_mix_post[l]), row(g_ffn_pre[l]), w_ff1[l].astype(BF16), w_ff2[l].astype(BF16),
              row(g_ffn_post[l]))
    out = _post(x2, og, ys.reshape(bsz * seq, WIDTH), sg, ss, consts, nb=bsz, rows_per_b=seq, tm=tm)
    return out.reshape(bsz, seq, dm)
```

```python
import functools

import jax
import jax.numpy as jnp
from jax import lax
from jax.experimental import pallas as pl
from jax.experimental.pallas import tpu as pltpu

F32 = jnp.float32
BF16 = jnp.bfloat16

D_MODEL = 1024
N_META = 16
CHUNK = 64
N_PAD = CHUNK - N_META
HEADS = 4
DK = 128
DV = 256
KEY = HEADS * DK
VAL = HEADS * DV
RANK = 16
TAU = 16.0
GROUPS = 64
GSIZE = 16
NSTATE = 64
WIDTH = GROUPS * GSIZE
D_FF = 4 * D_MODEL
EPS = 1e-6

LANE = 128
SUB = 8
SLAB = 256
SLAB_GROUPS = SLAB // GSIZE
N_SLAB = WIDTH // SLAB
SLAB_STATE = SLAB_GROUPS * NSTATE
N_TILE = SLAB_STATE // LANE
TILE2 = 2 * LANE
VMEM_LIMIT = 56 * 1024 * 1024


def _const_spec(shape):
    nd = len(shape)
    return pl.BlockSpec(shape, lambda *_: (0,) * nd, pipeline_mode=pl.Buffered(1))


def _rms(x, g):
    ms = jnp.mean(x * x, axis=-1, keepdims=True)
    return x * lax.rsqrt(ms + EPS) * g


def _sigmoid(x):
    return 1.0 / (1.0 + jnp.exp(-x))


def _dot(a, b):
    return jnp.dot(a, b, preferred_element_type=F32)


PROJ_SPLIT = 2


def _inproj_kernel(x_ref, g_ref, wq, wk, wv, wr, wa, wu, wzg, wzs, wup, bg,
                   q_o, k_o, v_o, r_o, la_o, u_o, zg_o, zs_o):
    xn = _rms(x_ref[...], g_ref[...]).astype(BF16)

    def mm(w):
        return _dot(xn, w[...])

    q_o[...] = (mm(wq) * (DK ** -0.5)).astype(BF16)
    k_o[...] = mm(wk).astype(BF16)
    v_o[...] = mm(wv).astype(BF16)
    r = mm(wr)
    r_o[...] = (r * _sigmoid(r)).astype(BF16)
    z = _dot(mm(wa).astype(BF16), wup[...]) + bg[...]
    la_o[...] = (jnp.minimum(z, 0.0) - jnp.log(1.0 + jnp.exp(-jnp.abs(z)))) * (1.0 / TAU)

    def piece(w, o, act, c):
        cols = slice(c * (D_MODEL // PROJ_SPLIT), (c + 1) * (D_MODEL // PROJ_SPLIT))
        o[:, cols] = act(_dot(xn, w[:, cols])).astype(BF16)

    return [functools.partial(piece, w, o, act, c)
            for w, o, act in ((wu, u_o, lambda t: t), (wzg, zg_o, _sigmoid), (wzs, zs_o, _sigmoid))
            for c in range(PROJ_SPLIT)]


def _gla_kernel(q_ref, k_ref, v_ref, la_ref, r_ref, gn_ref, s0_ref, o_ref, sf_ref,
                s_scr, b_scr, qi_scr, ki_scr, qs_scr, ks_scr, dc_scr, kv_scr, sc_scr, sb_scr,
                *, n_chunks, fillers=()):
    fillers = list(fillers)

    def fill(n):
        for _ in range(min(n, len(fillers))):
            fillers.pop(0)()

    j = pl.program_id(1)

    @pl.when(j == 0)
    def _():
        s_scr[...] = s0_ref[...]

    row = lax.broadcasted_iota(jnp.int32, (CHUNK, CHUNK), 0)
    col = lax.broadcasted_iota(jnp.int32, (CHUNK, CHUNK), 1)
    causal = row >= col
    tril = jnp.where(causal, 1.0, 0.0).astype(BF16)
    tn = (((0,), (0,)), ((), ()))
    nt = (((1,), (1,)), ((), ()))

    crow = lambda c: slice(c * CHUNK, (c + 1) * CHUNK)
    ksl = lambda h: slice(h * DK, (h + 1) * DK)
    vsl = lambda h: slice(h * DV, (h + 1) * DV)

    fill(1)
    for c in range(n_chunks):
        la = la_ref[crow(c), :]
        la_hi = la.astype(BF16)
        la_lo = (la - la_hi.astype(F32)).astype(BF16)
        b_scr[crow(c), :] = _dot(tril, la_hi) + _dot(tril, la_lo)

    fill(1)
    for c in range(n_chunks):
        b = b_scr[crow(c), :]
        b_mid = b[CHUNK // 2:CHUNK // 2 + 1, :]
        b_last = b[CHUNK - 1:CHUNK, :]
        q_in = q_ref[crow(c), :].astype(F32) * jnp.exp(b - b_mid)
        k_in = k_ref[crow(c), :].astype(F32) * jnp.exp(b_mid - b)
        qs_scr[crow(c), :] = (q_in * jnp.exp(b_mid)).astype(BF16)
        ks_scr[crow(c), :] = (k_in * jnp.exp(b_last - b_mid)).astype(BF16)
        qi_scr[crow(c), :] = q_in.astype(BF16)
        ki_scr[crow(c), :] = k_in.astype(BF16)
        dc_scr[c] = jnp.broadcast_to(jnp.exp(b_last), (SUB, KEY))

    fill(1)
    for c in range(n_chunks):
        for h in range(HEADS):
            kv_scr[c, h] = lax.dot_general(ks_scr[crow(c), ksl(h)], v_ref[crow(c), vsl(h)], tn,
                                           preferred_element_type=F32)
    fill(1)
    for c in range(n_chunks):
        for h in range(HEADS):
            sc = lax.dot_general(qi_scr[crow(c), ksl(h)], ki_scr[crow(c), ksl(h)], nt,
                                 preferred_element_type=F32)
            sc_scr[c, h] = jnp.where(causal, sc, 0.0).astype(BF16)

    fill(1)
    for c in range(n_chunks):
        for h in range(HEADS):
            s_old = s_scr[h]
            sb_scr[c, h] = s_old.astype(BF16)
            dcb = jnp.broadcast_to(dc_scr[c, 0:1, ksl(h)], (DK, DK)).T
            s_scr[h] = jnp.concatenate([dcb, dcb], axis=1) * s_old + kv_scr[c, h]

    fill(1)
    for c in range(n_chunks):
        for h in range(HEADS):
            lhs = jnp.concatenate([qs_scr[crow(c), ksl(h)], sc_scr[c, h]], axis=1)
            rhs = jnp.concatenate([sb_scr[c, h], v_ref[crow(c), vsl(h)]], axis=0)
            o = _dot(lhs, rhs)
            ms = jnp.mean(o * o, axis=-1, keepdims=True)
            og = o * lax.rsqrt(ms + EPS) * gn_ref[:, vsl(h)] * r_ref[crow(c), vsl(h)].astype(F32)
            o_ref[crow(c), vsl(h)] = og.astype(BF16)
    fill(len(fillers))

    @pl.when(j == pl.num_programs(1) - 1)
    def _():
        sf_ref[...] = s_scr[...]


def _mix_kernel(x_ref, g_ref, wq, wk, wv, wr, wa, wu, wzg, wzs, wup, bg, gn_ref, s0_ref,
                og_o, u_o, zg_o, zs_o, sf_ref, q_s, k_s, v_s, r_s, la_s, *gla_scratch,
                n_chunks, n_pad):
    deferred = _inproj_kernel(x_ref, g_ref, wq, wk, wv, wr, wa, wu, wzg, wzs, wup, bg,
                              q_s, k_s, v_s, r_s, la_s, u_o, zg_o, zs_o)
    if n_pad:
        la_s[0:n_pad, :] = jnp.zeros((n_pad, KEY), F32)
    _gla_kernel(q_s, k_s, v_s, la_s, r_s, gn_ref, s0_ref, og_o, sf_ref, *gla_scratch,
                n_chunks=n_chunks, fillers=deferred)


def _mix(x2, g, ws, wup, bg, gn, s0, *, nb, rows_per_b, tm, n_pad):
    nc = tm // CHUNK
    nt = rows_per_b // tm
    rows = nb * rows_per_b
    row_spec = lambda w: pl.BlockSpec((tm, w), lambda b, i: (b * nt + i, 0))
    state_spec = pl.BlockSpec((None, HEADS, DK, DV), lambda b, i: (b, 0, 0, 0))
    s0_spec = pl.BlockSpec((None, HEADS, DK, DV), lambda b, i: (0, 0, 0, 0))
    assert s0.shape == (1, HEADS, DK, DV)
    in_specs = [row_spec(D_MODEL), _const_spec(g.shape)]
    in_specs += [_const_spec(w.shape) for w in ws]
    in_specs += [_const_spec(wup.shape), _const_spec(bg.shape), _const_spec(gn.shape), s0_spec]
    wide = jax.ShapeDtypeStruct((rows, D_MODEL), BF16)
    return pl.pallas_call(
        functools.partial(_mix_kernel, n_chunks=nc, n_pad=n_pad),
        grid=(nb, nt),
        in_specs=in_specs,
        out_specs=[row_spec(VAL), row_spec(WIDTH), row_spec(D_MODEL), row_spec(D_MODEL), state_spec],
        out_shape=[wide, wide, wide, wide, jax.ShapeDtypeStruct((nb, HEADS, DK, DV), F32)],
        scratch_shapes=[pltpu.VMEM((tm, KEY), BF16), pltpu.VMEM((tm, KEY), BF16),
                        pltpu.VMEM((tm, VAL), BF16), pltpu.VMEM((tm, VAL), BF16),
                        pltpu.VMEM((tm, KEY), F32),
                        pltpu.VMEM((HEADS, DK, DV), F32),
                        pltpu.VMEM((tm, KEY), F32),
                        pltpu.VMEM((tm, KEY), BF16), pltpu.VMEM((tm, KEY), BF16),
                        pltpu.VMEM((tm, KEY), BF16), pltpu.VMEM((tm, KEY), BF16),
                        pltpu.VMEM((nc, SUB, KEY), F32),
                        pltpu.VMEM((nc, HEADS, DK, DV), F32),
                        pltpu.VMEM((nc, HEADS, CHUNK, CHUNK), BF16),
                        pltpu.VMEM((nc, HEADS, DK, DV), BF16)],
        name="mix",
        compiler_params=pltpu.CompilerParams(
            dimension_semantics=("parallel", "arbitrary"), vmem_limit_bytes=VMEM_LIMIT),
    )(x2, g, *ws, wup, bg, gn, s0)


def _lambda_bar(ar, ai, ls):
    dt = jnp.exp(ls)
    mag = jnp.exp(ar * dt)
    return mag * jnp.cos(ai * dt), mag * jnp.sin(ai * dt)


def _s5disc_kernel(ars_ref, ais_ref, lss_ref, ar_ref, ai_ref, ls_ref, br_ref, bi_ref, cr_ref, ci_ref,
                   lre_o, lim_o, bblk_o, cblk_o):
    lre_s, lim_s = _lambda_bar(ars_ref[...], ais_ref[...], lss_ref[...])
    lre_o[...] = lre_s
    lim_o[...] = lim_s

    ar = ar_ref[...]
    ai = ai_ref[...]
    lre, lim = _lambda_bar(ar, ai, ls_ref[...])
    zr = lre - 1.0
    zi = lim
    den = ar * ar + ai * ai
    fr = (zr * ar + zi * ai) / den
    fi = (zi * ar - zr * ai) / den
    br = br_ref[...]
    bi = bi_ref[...]
    bb = (fr * br - fi * bi, fr * bi + fi * br)

    grp_of_row = lax.broadcasted_iota(jnp.int32, (SLAB, LANE), 0) // GSIZE
    half_of_lane = lax.broadcasted_iota(jnp.int32, (SLAB, LANE), 1) // NSTATE
    grp_of_col = lax.broadcasted_iota(jnp.int32, (LANE, SLAB), 1) // GSIZE
    half_of_row = lax.broadcasted_iota(jnp.int32, (LANE, SLAB), 0) // NSTATE
    for q in range(N_SLAB):
        ch = slice(q * SLAB, (q + 1) * SLAB)
        b_two = [jnp.concatenate([p[ch, :], p[ch, :]], axis=1) for p in bb]
        c_two = [jnp.concatenate([p[:, ch], p[:, ch]], axis=0)
                 for p in (cr_ref, ci_ref)]
        c_two[1] = -c_two[1]
        for j in range(N_TILE):
            own_b = grp_of_row == 2 * j + half_of_lane
            own_c = grp_of_col == 2 * j + half_of_row
            for p in range(2):
                cols = slice(j * TILE2 + p * LANE, j * TILE2 + (p + 1) * LANE)
                bblk_o[q, :, cols] = jnp.where(own_b, b_two[p], 0.0).astype(BF16)
                cblk_o[q, cols, :] = jnp.where(own_c, c_two[p], 0.0).astype(BF16)


def _s5disc(ars, ais, lss, ar, ai, ls, br, bi, cr, ci):
    lam = jax.ShapeDtypeStruct((N_SLAB, SLAB_STATE), F32)
    return pl.pallas_call(
        _s5disc_kernel,
        out_shape=[lam, lam, jax.ShapeDtypeStruct((N_SLAB, SLAB, 2 * SLAB_STATE), BF16),
                   jax.ShapeDtypeStruct((N_SLAB, 2 * SLAB_STATE, SLAB), BF16)],
        name="s5disc",
    )(ars, ais, lss, ar, ai, ls, br, bi, cr, ci)


S5_T = 64
S5_PITCH = S5_T + 4
SCAN_TILES = 4
SCAN_GROUP = 8


def _s5_kernel(u_ref, bblk_ref, cblk_ref, lre_ref, lim_ref, d_ref, h0_ref, y_ref, hf_ref,
               stg_in, lhs_scr, bu_scr, s_scr, acc_scr, stg_out, st_scr):
    i = pl.program_id(0)

    @pl.when(i == 0)
    def _():
        st_scr[...] = h0_ref[...]

    T, P = S5_T, S5_PITCH
    lane_tile = lambda lt: slice(lt * LANE, (lt + 1) * LANE)

    def regroup_in(lt):
        for b in range(SUB):
            stg_in[lt, b * P:b * P + T, :] = u_ref[b, :, lane_tile(lt)].astype(F32)
        for tp in range(T // 2):
            x0 = stg_in[lt, pl.ds(2 * tp, SUB, stride=P), :]
            x1 = stg_in[lt, pl.ds(2 * tp + 1, SUB, stride=P), :]
            lhs_scr[2 * SUB * tp:2 * SUB * (tp + 1), lane_tile(lt)] = (
                jnp.concatenate([x0, x1], axis=0).astype(BF16))

    def in_piece(q, j):
        cols = slice(j * TILE2, (j + 1) * TILE2)
        bu_scr[q % 2, :, cols] = _dot(lhs_scr[:, q * SLAB:(q + 1) * SLAB], bblk_ref[q, :, cols])

    def out_piece(q, j):
        cols = slice(j * TILE2, (j + 1) * TILE2)
        part = _dot(s_scr[q % 2, :, cols], cblk_ref[q, cols, :])
        if j == 0:
            acc_scr[q % 2] = part
        else:
            acc_scr[q % 2] += part

    def scan_steps(q, tiles, lr, li, state, t0):
        for t in range(t0, t0 + SCAN_GROUP, 2):
            for j in tiles:
                re_c = slice(j * TILE2, j * TILE2 + LANE)
                im_c = slice(j * TILE2 + LANE, (j + 1) * TILE2)
                s_re, s_im = state[j]
                o_re, o_im = [], []
                for dt in range(2):
                    rows = slice(SUB * (t + dt), SUB * (t + dt + 1))
                    n_re = lr[j] * s_re - li[j] * s_im + bu_scr[q % 2, rows, re_c]
                    n_im = lr[j] * s_im + li[j] * s_re + bu_scr[q % 2, rows, im_c]
                    s_re, s_im = n_re, n_im
                    o_re.append(n_re)
                    o_im.append(n_im)
                state[j] = (s_re, s_im)
                rows2 = slice(SUB * t, SUB * (t + 2))
                s_scr[q % 2, rows2, re_c] = jnp.concatenate(o_re, axis=0).astype(BF16)
                s_scr[q % 2, rows2, im_c] = jnp.concatenate(o_im, axis=0).astype(BF16)

    def regroup_out(q):
        for l2 in range(SLAB // LANE):
            lt = q * (SLAB // LANE) + l2
            for t in range(T):
                stg_out[lt, pl.ds(t, SUB, stride=P), :] = (
                    acc_scr[q % 2, SUB * t:SUB * (t + 1), lane_tile(l2)])
            for b in range(SUB):
                y = (stg_out[lt, b * P:b * P + T, :]
                     + d_ref[:, lane_tile(lt)] * u_ref[b, :, lane_tile(lt)].astype(F32))
                y_ref[b, :, lane_tile(lt)] = y.astype(BF16)

    groups_per_run = T // SCAN_GROUP
    n_groups = groups_per_run * (N_TILE // SCAN_TILES)
    lt_per_slab = SLAB // LANE
    for lt in range(lt_per_slab):
        regroup_in(lt)
    for stage in range(N_SLAB + 2):
        q_in, q_scan, q_out = stage, stage - 1, stage - 2
        pieces = []
        for j in range(N_TILE):
            if q_in < N_SLAB:
                pieces.append(functools.partial(in_piece, q_in, j))
            if 0 <= q_out < N_SLAB:
                pieces.append(functools.partial(out_piece, q_out, j))
            if stage == 0 and j < WIDTH // LANE - lt_per_slab:
                pieces.append(functools.partial(regroup_in, lt_per_slab + j))
        scanning = 0 <= q_scan < N_SLAB
        for g in range(n_groups):
            for p in pieces[g * len(pieces) // n_groups:(g + 1) * len(pieces) // n_groups]:
                p()
            if not scanning:
                continue
            run, g_in_run = divmod(g, groups_per_run)
            tiles = range(run * SCAN_TILES, (run + 1) * SCAN_TILES)
            if g_in_run == 0:
                lr = {j: jnp.broadcast_to(lre_ref[q_scan:q_scan + 1, lane_tile(j)], (SUB, LANE))
                      for j in tiles}
                li = {j: jnp.broadcast_to(lim_ref[q_scan:q_scan + 1, lane_tile(j)], (SUB, LANE))
                      for j in tiles}
                state = {j: (st_scr[q_scan, :, j * TILE2:j * TILE2 + LANE],
                             st_scr[q_scan, :, j * TILE2 + LANE:(j + 1) * TILE2]) for j in tiles}
            scan_steps(q_scan, tiles, lr, li, state, g_in_run * SCAN_GROUP)
            if g_in_run == groups_per_run - 1:
                for j in tiles:
                    st_scr[q_scan, :, j * TILE2:j * TILE2 + LANE] = state[j][0]
                    st_scr[q_scan, :, j * TILE2 + LANE:(j + 1) * TILE2] = state[j][1]
        if 0 <= q_out < N_SLAB:
            regroup_out(q_out)

    @pl.when(i == pl.num_programs(0) - 1)
    def _():
        hf_ref[...] = st_scr[...]


def _s5(u3, bblk, cblk, lre, lim, d, h0):
    nb, t_total, _ = u3.shape
    assert nb == SUB and t_total % S5_T == 0
    blk = pl.BlockSpec((SUB, S5_T, WIDTH), lambda i: (0, i, 0))
    rows = SUB * S5_T
    n_lt = WIDTH // LANE
    return pl.pallas_call(
        _s5_kernel,
        grid=(t_total // S5_T,),
        in_specs=[blk, _const_spec(bblk.shape), _const_spec(cblk.shape),
                  _const_spec(lre.shape), _const_spec(lim.shape), _const_spec(d.shape),
                  _const_spec(h0.shape)],
        out_specs=[blk, _const_spec(h0.shape)],
        out_shape=[jax.ShapeDtypeStruct(u3.shape, BF16), jax.ShapeDtypeStruct(h0.shape, F32)],
        scratch_shapes=[pltpu.VMEM((n_lt, SUB * S5_PITCH, LANE), F32),
                        pltpu.VMEM((rows, WIDTH), BF16),
                        pltpu.VMEM((2, rows, 2 * SLAB_STATE), F32),
                        pltpu.VMEM((2, rows, 2 * SLAB_STATE), BF16),
                        pltpu.VMEM((2, rows, SLAB), F32),
                        pltpu.VMEM((n_lt, SUB * S5_PITCH, LANE), F32),
                        pltpu.VMEM(h0.shape, F32)],
        name="s5",
        compiler_params=pltpu.CompilerParams(
            dimension_semantics=("arbitrary",), vmem_limit_bytes=VMEM_LIMIT),
    )(u3, bblk, cblk, lre, lim, d, h0)


FF_SPLIT = 4


def _post_kernel(x_ref, og_ref, ys_ref, sg_ref, ss_ref, wo, wglu, bglu, wout, gpost, gffn,
                 w1, w2, gfpost, out_ref):
    y_gla = _dot(og_ref[...], wo[...])
    ys = ys_ref[...].astype(F32)
    gs = 0.5 * ys * (1.0 + jnp.tanh(0.7978845608028654 * (ys + 0.044715 * (ys * ys * ys))))
    glu = _dot(gs.astype(BF16), wglu[...]) + bglu[...]
    y_ssm = glu[:, :D_MODEL] * _sigmoid(glu[:, D_MODEL:])
    mixed = sg_ref[...].astype(F32) * y_gla + ss_ref[...].astype(F32) * y_ssm
    h1 = x_ref[...] + _rms(_dot(mixed.astype(BF16), wout[...]), gpost[...])
    hn = _rms(h1, gffn[...]).astype(BF16)
    wcols = D_FF // FF_SPLIT
    f = None
    for c in range(FF_SPLIT):
        a = jnp.maximum(_dot(hn, w1[:, c * wcols:(c + 1) * wcols]), 0.0)
        part = _dot((a * a).astype(BF16), w2[c * wcols:(c + 1) * wcols, :])
        f = part if f is None else f + part
    out_ref[...] = h1 + _rms(f, gfpost[...])


def _post(x2, og, ys, sg, ss, consts, *, nb, rows_per_b, tm):
    nt = rows_per_b // tm
    row_spec = lambda w: pl.BlockSpec((tm, w), lambda b, i: (b * nt + i, 0))
    in_specs = [row_spec(D_MODEL), row_spec(VAL), row_spec(WIDTH), row_spec(D_MODEL), row_spec(D_MODEL)]
    in_specs += [_const_spec(c.shape) for c in consts]
    return pl.pallas_call(
        _post_kernel, grid=(nb, nt), in_specs=in_specs, out_specs=row_spec(D_MODEL),
        out_shape=jax.ShapeDtypeStruct((nb * rows_per_b, D_MODEL), F32), name="post",
        compiler_params=pltpu.CompilerParams(
            dimension_semantics=("parallel", "parallel"), vmem_limit_bytes=VMEM_LIMIT),
    )(x2, og, ys, sg, ss, *consts)


def kernel(x, meta_tokens, g_mix_pre, w_in, w_gate_up, b_gate, gla_norm_g, w_o_gla, a_re, a_im,
           log_step, b_re, b_im, c_re, c_im, d_skip, w_glu, b_glu, w_out, g_mix_post, g_ffn_pre,
           w_ff1, w_ff2, g_ffn_post):
    bsz, seq, dm = x.shape
    assert dm == D_MODEL and bsz == SUB and seq % CHUNK == 0
    assert w_in.shape[0] == 1, "single layer"
    l = 0
    row = lambda t: t.reshape(1, -1).astype(F32)

    w = w_in[l]
    off = [0, KEY, 2 * KEY, 2 * KEY + VAL, 2 * KEY + 2 * VAL]
    wq = w[:, off[0]:off[1]].astype(BF16)
    wk = w[:, off[1]:off[2]].astype(BF16)
    wv = w[:, off[2]:off[3]].astype(BF16)
    wr = w[:, off[3]:off[4]].astype(BF16)
    a0 = off[4]
    wa = jnp.pad(w[:, a0:a0 + RANK], ((0, 0), (0, LANE - RANK))).astype(BF16)
    u0 = a0 + RANK
    wu = w[:, u0:u0 + WIDTH].astype(BF16)
    wzg = w[:, u0 + WIDTH:u0 + WIDTH + D_MODEL].astype(BF16)
    wzs = w[:, u0 + WIDTH + D_MODEL:].astype(BF16)
    ws = (wq, wk, wv, wr, wa, wu, wzg, wzs)
    wup = jnp.pad(w_gate_up[l], ((0, LANE - RANK), (0, 0))).astype(BF16)
    bg = row(b_gate[l])
    gpre = row(g_mix_pre[l])
    gn = row(gla_norm_g[l])

    rep = lambda t: jnp.repeat(t.astype(F32), GSIZE, axis=0)
    slabs = lambda t: t.astype(F32).reshape(N_SLAB, SLAB_STATE)
    rows_gh = lambda t: jnp.swapaxes(t, 1, 2).reshape(WIDTH, NSTATE).astype(F32)
    cols_gh = lambda t: jnp.transpose(t, (2, 0, 1)).reshape(NSTATE, WIDTH).astype(F32)
    ls_gn = jnp.broadcast_to(log_step[l][:, None], (GROUPS, NSTATE))
    lam_re, lam_im, bblk, cblk = _s5disc(
        slabs(a_re[l]), slabs(a_im[l]), slabs(ls_gn), rep(a_re[l]), rep(a_im[l]), rep(ls_gn),
        rows_gh(b_re[l]), rows_gh(b_im[l]), cols_gh(c_re[l]), cols_gh(c_im[l]))
    dsk = row(d_skip[l])

    meta_chunk = jnp.pad(meta_tokens.astype(F32), ((N_PAD, 0), (0, 0)))
    s_zero = jnp.zeros((1, HEADS, DK, DV), F32)
    _, mu, _, _, s_meta = _mix(meta_chunk, gpre, ws, wup, bg, gn, s_zero,
                               nb=1, rows_per_b=CHUNK, tm=CHUNK, n_pad=N_PAD)
    mu3 = jnp.broadcast_to(mu[None], (SUB, CHUNK, WIDTH))
    h_zero = jnp.zeros((N_SLAB, SUB, 2 * SLAB_STATE), F32)
    _, h_meta = _s5(mu3, bblk, cblk, lam_re, lam_im, dsk, h_zero)

    x2 = x.reshape(bsz * seq, dm)
    tm = 512 if seq % 512 == 0 else CHUNK
    og, u, sg, ss, _ = _mix(x2, gpre, ws, wup, bg, gn, s_meta, nb=bsz, rows_per_b=seq, tm=tm, n_pad=0)
    ys, _ = _s5(u.reshape(bsz, seq, WIDTH), bblk, cblk, lam_re, lam_im, dsk, h_meta)
    consts = (w_o_gla[l].astype(BF16), w_glu[l].astype(BF16), row(b_glu[l]), w_out[l].astype(BF16),
              row(g_mix_post[l]), row(g_ffn_pre[l]), w_ff1[l].astype(BF16), w_ff2[l].astype(BF16),
              row(g_ffn_post[l]))
    out = _post(x2, og, ys.reshape(bsz * seq, WIDTH), sg, ss, consts, nb=bsz, rows_per_b=seq, tm=tm)
    return out.reshape(bsz, seq, dm)
```

```python
import functools

import jax
import jax.numpy as jnp
from jax import lax
from jax.experimental import pallas as pl
from jax.experimental.pallas import tpu as pltpu

F32 = jnp.float32
BF16 = jnp.bfloat16

D_MODEL = 1024
N_META = 16
CHUNK = 64
N_PAD = CHUNK - N_META
HEADS = 4
DK = 128
DV = 256
KEY = HEADS * DK
VAL = HEADS * DV
RANK = 16
TAU = 16.0
GROUPS = 64
GSIZE = 16
NSTATE = 64
WIDTH = GROUPS * GSIZE
D_FF = 4 * D_MODEL
EPS = 1e-6

LANE = 128
SUB = 8
SLAB = 256
SLAB_GROUPS = SLAB // GSIZE
N_SLAB = WIDTH // SLAB
SLAB_STATE = SLAB_GROUPS * NSTATE
N_TILE = SLAB_STATE // LANE
TILE2 = 2 * LANE
VMEM_LIMIT = 56 * 1024 * 1024


def _const_spec(shape):
    nd = len(shape)
    return pl.BlockSpec(shape, lambda *_: (0,) * nd, pipeline_mode=pl.Buffered(1))


def _rms(x, g):
    ms = jnp.mean(x * x, axis=-1, keepdims=True)
    return x * lax.rsqrt(ms + EPS) * g


def _sigmoid(x):
    return 1.0 / (1.0 + jnp.exp(-x))


def _dot(a, b):
    return jnp.dot(a, b, preferred_element_type=F32)


PROJ_SPLIT = 2
W_COLS = {}
_start = 0
for _name, _width in (("q", KEY), ("k", KEY), ("v", VAL), ("r", VAL), ("a", LANE), ("u", WIDTH),
                      ("zg", D_MODEL), ("zs", D_MODEL)):
    W_COLS[_name] = (_start, _width)
    _start += _width
W_PACKED = _start


def _mix_kernel(x_ref, g_ref, w_ref, wup, bg, gn_ref, s0_ref,
                o_ref, u_o, zg_o, zs_o, sf_ref,
                q_s, k_s, v_s, r_s, la_s,
                s_scr, b_scr, qi_scr, ki_scr, qs_scr, ks_scr, dc_scr, kv_scr, sc_scr, sb_scr,
                *, n_chunks, n_pad):
    j = pl.program_id(1)

    @pl.when(j == 0)
    def _():
        s_scr[...] = s0_ref[...]

    xn = _rms(x_ref[...], g_ref[...]).astype(BF16)

    def mm(name, lo=0, hi=None):
        start, width = W_COLS[name]
        hi = width if hi is None else hi
        return _dot(xn, w_ref[:, start + lo:start + hi])

    def proj_q():
        q_s[...] = (mm("q") * (DK ** -0.5)).astype(BF16)

    def proj_k():
        k_s[...] = mm("k").astype(BF16)

    def proj_v():
        v_s[...] = mm("v").astype(BF16)

    def proj_r():
        r = mm("r")
        r_s[...] = (r * _sigmoid(r)).astype(BF16)

    def proj_piece(name, o, act, c):
        lo, hi = c * (D_MODEL // PROJ_SPLIT), (c + 1) * (D_MODEL // PROJ_SPLIT)
        o[:, lo:hi] = act(mm(name, lo, hi)).astype(BF16)

    late = [functools.partial(proj_piece, name, o, act, c)
            for name, o, act in (("u", u_o, lambda t: t), ("zg", zg_o, _sigmoid), ("zs", zs_o, _sigmoid))
            for c in range(PROJ_SPLIT)]

    def fill_late(n):
        for _ in range(min(n, len(late))):
            late.pop(0)()

    z = _dot(mm("a").astype(BF16), wup[...]) + bg[...]
    la_s[...] = (jnp.minimum(z, 0.0) - jnp.log(1.0 + jnp.exp(-jnp.abs(z)))) * (1.0 / TAU)
    if n_pad:
        la_s[0:n_pad, :] = jnp.zeros((n_pad, KEY), F32)

    row = lax.broadcasted_iota(jnp.int32, (CHUNK, CHUNK), 0)
    col = lax.broadcasted_iota(jnp.int32, (CHUNK, CHUNK), 1)
    causal = row >= col
    tril = jnp.where(causal, 1.0, 0.0).astype(BF16)
    tn = (((0,), (0,)), ((), ()))
    nt = (((1,), (1,)), ((), ()))
    crow = lambda c: slice(c * CHUNK, (c + 1) * CHUNK)
    ksl = lambda h: slice(h * DK, (h + 1) * DK)
    vsl = lambda h: slice(h * DV, (h + 1) * DV)

    proj_q()
    for c in range(n_chunks):
        la = la_s[crow(c), :]
        la_hi = la.astype(BF16)
        la_lo = (la - la_hi.astype(F32)).astype(BF16)
        b_scr[crow(c), :] = _dot(tril, la_hi) + _dot(tril, la_lo)

    proj_k()
    for c in range(n_chunks):
        b = b_scr[crow(c), :]
        b_mid = b[CHUNK // 2:CHUNK // 2 + 1, :]
        b_last = b[CHUNK - 1:CHUNK, :]
        q_in = q_s[crow(c), :].astype(F32) * jnp.exp(b - b_mid)
        k_in = k_s[crow(c), :].astype(F32) * jnp.exp(b_mid - b)
        qs_scr[crow(c), :] = (q_in * jnp.exp(b_mid)).astype(BF16)
        ks_scr[crow(c), :] = (k_in * jnp.exp(b_last - b_mid)).astype(BF16)
        qi_scr[crow(c), :] = q_in.astype(BF16)
        ki_scr[crow(c), :] = k_in.astype(BF16)
        dc_scr[c] = jnp.broadcast_to(jnp.exp(b_last), (SUB, KEY))

    proj_v()
    for c in range(n_chunks):
        for h in range(HEADS):
            sc = lax.dot_general(qi_scr[crow(c), ksl(h)], ki_scr[crow(c), ksl(h)], nt,
                                 preferred_element_type=F32)
            sc_scr[c, h] = jnp.where(causal, sc, 0.0).astype(BF16)

    proj_r()
    for c in range(n_chunks):
        for h in range(HEADS):
            kv_scr[c, h] = lax.dot_general(ks_scr[crow(c), ksl(h)], v_s[crow(c), vsl(h)], tn,
                                           preferred_element_type=F32)

    fill_late(1)
    for c in range(n_chunks):
        for h in range(HEADS):
            s_old = s_scr[h]
            sb_scr[c, h] = s_old.astype(BF16)
            dcb = jnp.broadcast_to(dc_scr[c, 0:1, ksl(h)], (DK, DK)).T
            s_scr[h] = jnp.concatenate([dcb, dcb], axis=1) * s_old + kv_scr[c, h]

    fill_late(1)
    for c in range(n_chunks):
        for h in range(HEADS):
            lhs = jnp.concatenate([qs_scr[crow(c), ksl(h)], sc_scr[c, h]], axis=1)
            rhs = jnp.concatenate([sb_scr[c, h], v_s[crow(c), vsl(h)]], axis=0)
            o = _dot(lhs, rhs)
            ms = jnp.mean(o * o, axis=-1, keepdims=True)
            og = o * lax.rsqrt(ms + EPS) * gn_ref[:, vsl(h)] * r_s[crow(c), vsl(h)].astype(F32)
            o_ref[crow(c), vsl(h)] = og.astype(BF16)
        if c % 2 == 1:
            fill_late(1)
    fill_late(len(late))

    @pl.when(j == pl.num_programs(1) - 1)
    def _():
        sf_ref[...] = s_scr[...]


def _mix(x2, g, w_packed, wup, bg, gn, s0, *, nb, rows_per_b, tm, n_pad):
    nc = tm // CHUNK
    nt = rows_per_b // tm
    rows = nb * rows_per_b
    row_spec = lambda w: pl.BlockSpec((tm, w), lambda b, i: (b * nt + i, 0))
    state_spec = pl.BlockSpec((None, HEADS, DK, DV), lambda b, i: (b, 0, 0, 0))
    s0_spec = pl.BlockSpec((None, HEADS, DK, DV), lambda b, i: (0, 0, 0, 0))
    assert s0.shape == (1, HEADS, DK, DV)
    in_specs = [row_spec(D_MODEL), _const_spec(g.shape), _const_spec(w_packed.shape),
                _const_spec(wup.shape), _const_spec(bg.shape), _const_spec(gn.shape), s0_spec]
    wide = jax.ShapeDtypeStruct((rows, D_MODEL), BF16)
    return pl.pallas_call(
        functools.partial(_mix_kernel, n_chunks=nc, n_pad=n_pad),
        grid=(nb, nt),
        in_specs=in_specs,
        out_specs=[row_spec(VAL), row_spec(WIDTH), row_spec(D_MODEL), row_spec(D_MODEL), state_spec],
        out_shape=[wide, wide, wide, wide, jax.ShapeDtypeStruct((nb, HEADS, DK, DV), F32)],
        scratch_shapes=[pltpu.VMEM((tm, KEY), BF16), pltpu.VMEM((tm, KEY), BF16),
                        pltpu.VMEM((tm, VAL), BF16), pltpu.VMEM((tm, VAL), BF16),
                        pltpu.VMEM((tm, KEY), F32),
                        pltpu.VMEM((HEADS, DK, DV), F32),
                        pltpu.VMEM((tm, KEY), F32),
                        pltpu.VMEM((tm, KEY), BF16), pltpu.VMEM((tm, KEY), BF16),
                        pltpu.VMEM((tm, KEY), BF16), pltpu.VMEM((tm, KEY), BF16),
                        pltpu.VMEM((nc, SUB, KEY), F32),
                        pltpu.VMEM((nc, HEADS, DK, DV), F32),
                        pltpu.VMEM((nc, HEADS, CHUNK, CHUNK), BF16),
                        pltpu.VMEM((nc, HEADS, DK, DV), BF16)],
        name="mix",
        compiler_params=pltpu.CompilerParams(
            dimension_semantics=("parallel", "arbitrary"), vmem_limit_bytes=VMEM_LIMIT),
    )(x2, g, w_packed, wup, bg, gn, s0)


def _lambda_bar(ar, ai, ls):
    dt = jnp.exp(ls)
    mag = jnp.exp(ar * dt)
    return mag * jnp.cos(ai * dt), mag * jnp.sin(ai * dt)


def _s5disc_kernel(ars_ref, ais_ref, lss_ref, ar_ref, ai_ref, ls_ref, br_ref, bi_ref, cr_ref, ci_ref,
                   lre_o, lim_o, bblk_o, cblk_o):
    lre_s, lim_s = _lambda_bar(ars_ref[...], ais_ref[...], lss_ref[...])
    lre_o[...] = lre_s
    lim_o[...] = lim_s

    ar = ar_ref[...]
    ai = ai_ref[...]
    lre, lim = _lambda_bar(ar, ai, ls_ref[...])
    zr = lre - 1.0
    zi = lim
    den = ar * ar + ai * ai
    fr = (zr * ar + zi * ai) / den
    fi = (zi * ar - zr * ai) / den
    br = br_ref[...]
    bi = bi_ref[...]
    bb = (fr * br - fi * bi, fr * bi + fi * br)

    grp_of_row = lax.broadcasted_iota(jnp.int32, (SLAB, LANE), 0) // GSIZE
    half_of_lane = lax.broadcasted_iota(jnp.int32, (SLAB, LANE), 1) // NSTATE
    grp_of_col = lax.broadcasted_iota(jnp.int32, (LANE, SLAB), 1) // GSIZE
    half_of_row = lax.broadcasted_iota(jnp.int32, (LANE, SLAB), 0) // NSTATE
    for q in range(N_SLAB):
        ch = slice(q * SLAB, (q + 1) * SLAB)
        b_two = [jnp.concatenate([p[ch, :], p[ch, :]], axis=1) for p in bb]
        c_two = [jnp.concatenate([p[:, ch], p[:, ch]], axis=0)
                 for p in (cr_ref, ci_ref)]
        c_two[1] = -c_two[1]
        for j in range(N_TILE):
            own_b = grp_of_row == 2 * j + half_of_lane
            own_c = grp_of_col == 2 * j + half_of_row
            for p in range(2):
                cols = slice(j * TILE2 + p * LANE, j * TILE2 + (p + 1) * LANE)
                bblk_o[q, :, cols] = jnp.where(own_b, b_two[p], 0.0).astype(BF16)
                cblk_o[q, cols, :] = jnp.where(own_c, c_two[p], 0.0).astype(BF16)


def _s5disc(ars, ais, lss, ar, ai, ls, br, bi, cr, ci):
    lam = jax.ShapeDtypeStruct((N_SLAB, SLAB_STATE), F32)
    return pl.pallas_call(
        _s5disc_kernel,
        out_shape=[lam, lam, jax.ShapeDtypeStruct((N_SLAB, SLAB, 2 * SLAB_STATE), BF16),
                   jax.ShapeDtypeStruct((N_SLAB, 2 * SLAB_STATE, SLAB), BF16)],
        name="s5disc",
    )(ars, ais, lss, ar, ai, ls, br, bi, cr, ci)


S5_T = 64
S5_PAD = 4
SCAN_TILES = 4
SCAN_GROUP = 8


def _s5_kernel(u_ref, bblk_ref, cblk_ref, lre_ref, lim_ref, d_ref, h0_ref, y_ref, hf_ref,
               stg_in, lhs_scr, bu_scr, s_scr, acc_scr, stg_out, st_scr, *, T):
    i = pl.program_id(0)

    @pl.when(i == 0)
    def _():
        st_scr[...] = h0_ref[...]

    P = T + S5_PAD
    lane_tile = lambda lt: slice(lt * LANE, (lt + 1) * LANE)

    def regroup_in(lt):
        for b in range(SUB):
            stg_in[lt, b * P:b * P + T, :] = u_ref[b, :, lane_tile(lt)].astype(F32)
        for tp in range(T // 2):
            x0 = stg_in[lt, pl.ds(2 * tp, SUB, stride=P), :]
            x1 = stg_in[lt, pl.ds(2 * tp + 1, SUB, stride=P), :]
            lhs_scr[2 * SUB * tp:2 * SUB * (tp + 1), lane_tile(lt)] = (
                jnp.concatenate([x0, x1], axis=0).astype(BF16))

    def in_piece(q, j):
        cols = slice(j * TILE2, (j + 1) * TILE2)
        bu_scr[q % 2, :, cols] = _dot(lhs_scr[:, q * SLAB:(q + 1) * SLAB], bblk_ref[q, :, cols])

    def out_piece(q, j):
        cols = slice(j * TILE2, (j + 1) * TILE2)
        part = _dot(s_scr[q % 2, :, cols], cblk_ref[q, cols, :])
        if j == 0:
            acc_scr[q % 2] = part
        else:
            acc_scr[q % 2] += part

    def scan_steps(q, tiles, lr, li, state, t0):
        for t in range(t0, t0 + SCAN_GROUP, 2):
            for j in tiles:
                re_c = slice(j * TILE2, j * TILE2 + LANE)
                im_c = slice(j * TILE2 + LANE, (j + 1) * TILE2)
                s_re, s_im = state[j]
                o_re, o_im = [], []
                for dt in range(2):
                    rows = slice(SUB * (t + dt), SUB * (t + dt + 1))
                    n_re = lr[j] * s_re - li[j] * s_im + bu_scr[q % 2, rows, re_c]
                    n_im = lr[j] * s_im + li[j] * s_re + bu_scr[q % 2, rows, im_c]
                    s_re, s_im = n_re, n_im
                    o_re.append(n_re)
                    o_im.append(n_im)
                state[j] = (s_re, s_im)
                rows2 = slice(SUB * t, SUB * (t + 2))
                s_scr[q % 2, rows2, re_c] = jnp.concatenate(o_re, axis=0).astype(BF16)
                s_scr[q % 2, rows2, im_c] = jnp.concatenate(o_im, axis=0).astype(BF16)

    def regroup_out(q):
        for l2 in range(SLAB // LANE):
            lt = q * (SLAB // LANE) + l2
            for t in range(T):
                stg_out[lt, pl.ds(t, SUB, stride=P), :] = (
                    acc_scr[q % 2, SUB * t:SUB * (t + 1), lane_tile(l2)])
            for b in range(SUB):
                y = (stg_out[lt, b * P:b * P + T, :]
                     + d_ref[:, lane_tile(lt)] * u_ref[b, :, lane_tile(lt)].astype(F32))
                y_ref[b, :, lane_tile(lt)] = y.astype(BF16)

    groups_per_run = T // SCAN_GROUP
    n_groups = groups_per_run * (N_TILE // SCAN_TILES)
    lt_per_slab = SLAB // LANE
    for lt in range(lt_per_slab):
        regroup_in(lt)
    for stage in range(N_SLAB + 2):
        q_in, q_scan, q_out = stage, stage - 1, stage - 2
        pieces = []
        for j in range(N_TILE):
            if q_in < N_SLAB:
                pieces.append(functools.partial(in_piece, q_in, j))
            if 0 <= q_out < N_SLAB:
                pieces.append(functools.partial(out_piece, q_out, j))
            if stage == 0 and j < WIDTH // LANE - lt_per_slab:
                pieces.append(functools.partial(regroup_in, lt_per_slab + j))
        scanning = 0 <= q_scan < N_SLAB
        for g in range(n_groups):
            for p in pieces[g * len(pieces) // n_groups:(g + 1) * len(pieces) // n_groups]:
                p()
            if not scanning:
                continue
            run, g_in_run = divmod(g, groups_per_run)
            tiles = range(run * SCAN_TILES, (run + 1) * SCAN_TILES)
            if g_in_run == 0:
                lr = {j: jnp.broadcast_to(lre_ref[q_scan:q_scan + 1, lane_tile(j)], (SUB, LANE))
                      for j in tiles}
                li = {j: jnp.broadcast_to(lim_ref[q_scan:q_scan + 1, lane_tile(j)], (SUB, LANE))
                      for j in tiles}
                state = {j: (st_scr[q_scan, :, j * TILE2:j * TILE2 + LANE],
                             st_scr[q_scan, :, j * TILE2 + LANE:(j + 1) * TILE2]) for j in tiles}
            scan_steps(q_scan, tiles, lr, li, state, g_in_run * SCAN_GROUP)
            if g_in_run == groups_per_run - 1:
                for j in tiles:
                    st_scr[q_scan, :, j * TILE2:j * TILE2 + LANE] = state[j][0]
                    st_scr[q_scan, :, j * TILE2 + LANE:(j + 1) * TILE2] = state[j][1]
        if 0 <= q_out < N_SLAB:
            regroup_out(q_out)

    @pl.when(i == pl.num_programs(0) - 1)
    def _():
        hf_ref[...] = st_scr[...]


def _s5(u3, bblk, cblk, lre, lim, d, h0, *, t_blk):
    nb, t_total, _ = u3.shape
    assert nb == SUB and t_total % t_blk == 0 and t_blk % (2 * SCAN_GROUP) == 0
    assert ((t_blk + S5_PAD) // 4) % 2 == 1
    blk = pl.BlockSpec((SUB, t_blk, WIDTH), lambda i: (0, i, 0))
    rows = SUB * t_blk
    n_lt = WIDTH // LANE
    pitch = t_blk + S5_PAD
    return pl.pallas_call(
        functools.partial(_s5_kernel, T=t_blk),
        grid=(t_total // t_blk,),
        in_specs=[blk, _const_spec(bblk.shape), _const_spec(cblk.shape),
                  _const_spec(lre.shape), _const_spec(lim.shape), _const_spec(d.shape),
                  _const_spec(h0.shape)],
        out_specs=[blk, _const_spec(h0.shape)],
        out_shape=[jax.ShapeDtypeStruct(u3.shape, BF16), jax.ShapeDtypeStruct(h0.shape, F32)],
        scratch_shapes=[pltpu.VMEM((n_lt, SUB * pitch, LANE), F32),
                        pltpu.VMEM((rows, WIDTH), BF16),
                        pltpu.VMEM((2, rows, 2 * SLAB_STATE), F32),
                        pltpu.VMEM((2, rows, 2 * SLAB_STATE), BF16),
                        pltpu.VMEM((2, rows, SLAB), F32),
                        pltpu.VMEM((n_lt, SUB * pitch, LANE), F32),
                        pltpu.VMEM(h0.shape, F32)],
        name="s5",
        compiler_params=pltpu.CompilerParams(
            dimension_semantics=("arbitrary",), vmem_limit_bytes=VMEM_LIMIT),
    )(u3, bblk, cblk, lre, lim, d, h0)


FF_SPLIT = 4


POST_SPLIT = 2


def _post_kernel(x_ref, og_ref, ys_ref, sg_ref, ss_ref, wo, wglu, bglu, wout, gpost, gffn,
                 w1, w2, gfpost, out_ref):
    tm = x_ref.shape[0]
    sub = tm // POST_SPLIT
    rows = [slice(p * sub, (p + 1) * sub) for p in range(POST_SPLIT)]
    wcols = D_FF // FF_SPLIT
    st = [dict() for _ in rows]

    def s_gla(p, r):
        p["y_gla"] = _dot(og_ref[r, :], wo[...])

    def s_glu(p, r):
        ys = ys_ref[r, :].astype(F32)
        gs = 0.5 * ys * (1.0 + jnp.tanh(0.7978845608028654 * (ys + 0.044715 * (ys * ys * ys))))
        p["glu"] = _dot(gs.astype(BF16), wglu[...]) + bglu[...]

    def s_mix(p, r):
        glu = p.pop("glu")
        y_ssm = glu[:, :D_MODEL] * _sigmoid(glu[:, D_MODEL:])
        mixed = sg_ref[r, :].astype(F32) * p.pop("y_gla") + ss_ref[r, :].astype(F32) * y_ssm
        p["m2"] = _dot(mixed.astype(BF16), wout[...])

    def s_h1(p, r):
        p["h1"] = x_ref[r, :] + _rms(p.pop("m2"), gpost[...])
        p["hn"] = _rms(p["h1"], gffn[...]).astype(BF16)

    def s_ff(c):
        def run(p, r):
            a = jnp.maximum(_dot(p["hn"], w1[:, c * wcols:(c + 1) * wcols]), 0.0)
            part = _dot((a * a).astype(BF16), w2[c * wcols:(c + 1) * wcols, :])
            p["f"] = part if c == 0 else p["f"] + part
        return run

    def s_out(p, r):
        out_ref[r, :] = p["h1"] + _rms(p["f"], gfpost[...])

    for stage in [s_gla, s_glu, s_mix, s_h1] + [s_ff(c) for c in range(FF_SPLIT)] + [s_out]:
        for p, r in zip(st, rows):
            stage(p, r)


def _post(x2, og, ys, sg, ss, consts, *, nb, rows_per_b, tm):
    nt = rows_per_b // tm
    row_spec = lambda w: pl.BlockSpec((tm, w), lambda b, i: (b * nt + i, 0))
    in_specs = [row_spec(D_MODEL), row_spec(VAL), row_spec(WIDTH), row_spec(D_MODEL), row_spec(D_MODEL)]
    in_specs += [_const_spec(c.shape) for c in consts]
    return pl.pallas_call(
        _post_kernel, grid=(nb, nt), in_specs=in_specs, out_specs=row_spec(D_MODEL),
        out_shape=jax.ShapeDtypeStruct((nb * rows_per_b, D_MODEL), F32), name="post",
        compiler_params=pltpu.CompilerParams(
            dimension_semantics=("parallel", "parallel"), vmem_limit_bytes=VMEM_LIMIT),
    )(x2, og, ys, sg, ss, *consts)


def kernel(x, meta_tokens, g_mix_pre, w_in, w_gate_up, b_gate, gla_norm_g, w_o_gla, a_re, a_im,
           log_step, b_re, b_im, c_re, c_im, d_skip, w_glu, b_glu, w_out, g_mix_post, g_ffn_pre,
           w_ff1, w_ff2, g_ffn_post):
    bsz, seq, dm = x.shape
    assert dm == D_MODEL and bsz == SUB and seq % CHUNK == 0
    assert w_in.shape[0] == 1, "single layer"
    l = 0
    row = lambda t: t.reshape(1, -1).astype(F32)

    w = w_in[l]
    a0 = W_COLS["a"][0]
    w_packed = jnp.concatenate(
        [w[:, :a0], jnp.pad(w[:, a0:a0 + RANK], ((0, 0), (0, LANE - RANK))), w[:, a0 + RANK:]],
        axis=1).astype(BF16)
    assert w_packed.shape == (D_MODEL, W_PACKED)
    wup = jnp.pad(w_gate_up[l], ((0, LANE - RANK), (0, 0))).astype(BF16)
    bg = row(b_gate[l])
    gpre = row(g_mix_pre[l])
    gn = row(gla_norm_g[l])

    rep = lambda t: jnp.repeat(t.astype(F32), GSIZE, axis=0)
    slabs = lambda t: t.astype(F32).reshape(N_SLAB, SLAB_STATE)
    rows_gh = lambda t: jnp.swapaxes(t, 1, 2).reshape(WIDTH, NSTATE).astype(F32)
    cols_gh = lambda t: jnp.transpose(t, (2, 0, 1)).reshape(NSTATE, WIDTH).astype(F32)
    ls_gn = jnp.broadcast_to(log_step[l][:, None], (GROUPS, NSTATE))
    lam_re, lam_im, bblk, cblk = _s5disc(
        slabs(a_re[l]), slabs(a_im[l]), slabs(ls_gn), rep(a_re[l]), rep(a_im[l]), rep(ls_gn),
        rows_gh(b_re[l]), rows_gh(b_im[l]), cols_gh(c_re[l]), cols_gh(c_im[l]))
    dsk = row(d_skip[l])

    meta_chunk = jnp.pad(meta_tokens.astype(F32), ((N_PAD, 0), (0, 0)))
    s_zero = jnp.zeros((1, HEADS, DK, DV), F32)
    _, mu, _, _, s_meta = _mix(meta_chunk, gpre, w_packed, wup, bg, gn, s_zero,
                               nb=1, rows_per_b=CHUNK, tm=CHUNK, n_pad=N_PAD)
    mu3 = jnp.broadcast_to(mu[None, N_PAD:], (SUB, N_META, WIDTH))
    h_zero = jnp.zeros((N_SLAB, SUB, 2 * SLAB_STATE), F32)
    _, h_meta = _s5(mu3, bblk, cblk, lam_re, lam_im, dsk, h_zero, t_blk=N_META)

    x2 = x.reshape(bsz * seq, dm)
    tm = 512 if seq % 512 == 0 else CHUNK
    og, u, sg, ss, _ = _mix(x2, gpre, w_packed, wup, bg, gn, s_meta, nb=bsz, rows_per_b=seq, tm=tm, n_pad=0)
    ys, _ = _s5(u.reshape(bsz, seq, WIDTH), bblk, cblk, lam_re, lam_im, dsk, h_meta, t_blk=S5_T)
    consts = (w_o_gla[l].astype(BF16), w_glu[l].astype(BF16), row(b_glu[l]), w_out[l].astype(BF16),
              row(g_mix_post[l]), row(g_ffn_pre[l]), w_ff1[l].astype(BF16), w_ff2[l].astype(BF16),
              row(g_ffn_post[l]))
    out = _post(x2, og, ys.reshape(bsz * seq, WIDTH), sg, ss, consts, nb=bsz, rows_per_b=seq, tm=tm)
    return out.reshape(bsz, seq, dm)
```

```python
import functools

import jax
import jax.numpy as jnp
from jax import lax
from jax.experimental import pallas as pl
from jax.experimental.pallas import tpu as pltpu

F32 = jnp.float32
BF16 = jnp.bfloat16

D_MODEL = 1024
N_META = 16
CHUNK = 64
N_PAD = CHUNK - N_META
HEADS = 4
DK = 128
DV = 256
KEY = HEADS * DK
VAL = HEADS * DV
RANK = 16
TAU = 16.0
GROUPS = 64
GSIZE = 16
NSTATE = 64
WIDTH = GROUPS * GSIZE
D_FF = 4 * D_MODEL
EPS = 1e-6

LANE = 128
SUB = 8
SLAB = 256
SLAB_GROUPS = SLAB // GSIZE
N_SLAB = WIDTH // SLAB
SLAB_STATE = SLAB_GROUPS * NSTATE
N_TILE = SLAB_STATE // LANE
TILE2 = 2 * LANE
VMEM_LIMIT = 56 * 1024 * 1024


def _const_spec(shape):
    nd = len(shape)
    return pl.BlockSpec(shape, lambda *_: (0,) * nd, pipeline_mode=pl.Buffered(1))


def _rms(x, g):
    ms = jnp.mean(x * x, axis=-1, keepdims=True)
    return x * lax.rsqrt(ms + EPS) * g


def _sigmoid(x):
    return 1.0 / (1.0 + jnp.exp(-x))


def _dot(a, b):
    return jnp.dot(a, b, preferred_element_type=F32)


PROJ_SPLIT = 2
W_COLS = {"q": (0, 0, KEY), "k": (0, KEY, KEY), "v": (0, 2 * KEY, VAL), "r": (0, 2 * KEY + VAL, VAL),
          "a": (1, 0, LANE),
          "u": (2, 0, WIDTH), "zg": (2, WIDTH, D_MODEL), "zs": (2, WIDTH + D_MODEL, D_MODEL)}
W_HEAD = 2 * KEY + 2 * VAL


def _mix_kernel(x_ref, g_ref, w_head, w_a, w_tail, wup, bg, gn_ref, s0_ref,
                o_ref, u_o, zg_o, zs_o, sf_ref,
                q_s, k_s, v_s, r_s, la_s,
                s_scr, b_scr, qi_scr, ki_scr, qs_scr, ks_scr, dc_scr, kv_scr, sc_scr, sb_scr,
                *, n_chunks, n_pad):
    j = pl.program_id(1)

    @pl.when(j == 0)
    def _():
        s_scr[...] = s0_ref[...]

    xn = _rms(x_ref[...], g_ref[...]).astype(BF16)

    def mm(name, lo=0, hi=None):
        which, start, width = W_COLS[name]
        hi = width if hi is None else hi
        return _dot(xn, (w_head, w_a, w_tail)[which][:, start + lo:start + hi])

    def proj_q():
        q_s[...] = (mm("q") * (DK ** -0.5)).astype(BF16)

    def proj_k():
        k_s[...] = mm("k").astype(BF16)

    def proj_v():
        v_s[...] = mm("v").astype(BF16)

    def proj_r():
        r = mm("r")
        r_s[...] = (r * _sigmoid(r)).astype(BF16)

    def proj_piece(name, o, act, c):
        lo, hi = c * (D_MODEL // PROJ_SPLIT), (c + 1) * (D_MODEL // PROJ_SPLIT)
        o[:, lo:hi] = act(mm(name, lo, hi)).astype(BF16)

    late = [functools.partial(proj_piece, name, o, act, c)
            for name, o, act in (("u", u_o, lambda t: t), ("zg", zg_o, _sigmoid), ("zs", zs_o, _sigmoid))
            for c in range(PROJ_SPLIT)]

    def fill_late(n):
        for _ in range(min(n, len(late))):
            late.pop(0)()

    z = _dot(mm("a").astype(BF16), wup[...]) + bg[...]
    la_s[...] = (jnp.minimum(z, 0.0) - jnp.log(1.0 + jnp.exp(-jnp.abs(z)))) * (1.0 / TAU)
    if n_pad:
        la_s[0:n_pad, :] = jnp.zeros((n_pad, KEY), F32)

    row = lax.broadcasted_iota(jnp.int32, (CHUNK, CHUNK), 0)
    col = lax.broadcasted_iota(jnp.int32, (CHUNK, CHUNK), 1)
    causal = row >= col
    tril = jnp.where(causal, 1.0, 0.0).astype(BF16)
    tn = (((0,), (0,)), ((), ()))
    nt = (((1,), (1,)), ((), ()))
    crow = lambda c: slice(c * CHUNK, (c + 1) * CHUNK)
    ksl = lambda h: slice(h * DK, (h + 1) * DK)
    vsl = lambda h: slice(h * DV, (h + 1) * DV)

    proj_q()
    for c in range(n_chunks):
        la = la_s[crow(c), :]
        la_hi = la.astype(BF16)
        la_lo = (la - la_hi.astype(F32)).astype(BF16)
        b_scr[crow(c), :] = _dot(tril, la_hi) + _dot(tril, la_lo)

    proj_k()
    for c in range(n_chunks):
        b = b_scr[crow(c), :]
        b_mid = b[CHUNK // 2:CHUNK // 2 + 1, :]
        b_last = b[CHUNK - 1:CHUNK, :]
        q_in = q_s[crow(c), :].astype(F32) * jnp.exp(b - b_mid)
        k_in = k_s[crow(c), :].astype(F32) * jnp.exp(b_mid - b)
        qs_scr[crow(c), :] = (q_in * jnp.exp(b_mid)).astype(BF16)
        ks_scr[crow(c), :] = (k_in * jnp.exp(b_last - b_mid)).astype(BF16)
        qi_scr[crow(c), :] = q_in.astype(BF16)
        ki_scr[crow(c), :] = k_in.astype(BF16)
        dc_scr[c] = jnp.broadcast_to(jnp.exp(b_last), (SUB, KEY))

    proj_v()
    for c in range(n_chunks):
        for h in range(HEADS):
            sc = lax.dot_general(qi_scr[crow(c), ksl(h)], ki_scr[crow(c), ksl(h)], nt,
                                 preferred_element_type=F32)
            sc_scr[c, h] = jnp.where(causal, sc, 0.0).astype(BF16)

    proj_r()
    for c in range(n_chunks):
        for h in range(HEADS):
            kv_scr[c, h] = lax.dot_general(ks_scr[crow(c), ksl(h)], v_s[crow(c), vsl(h)], tn,
                                           preferred_element_type=F32)

    fill_late(1)
    for c in range(n_chunks):
        for h in range(HEADS):
            s_old = s_scr[h]
            sb_scr[c, h] = s_old.astype(BF16)
            dcb = jnp.broadcast_to(dc_scr[c, 0:1, ksl(h)], (DK, DK)).T
            s_scr[h] = jnp.concatenate([dcb, dcb], axis=1) * s_old + kv_scr[c, h]

    fill_late(1)
    for c in range(n_chunks):
        for h in range(HEADS):
            lhs = jnp.concatenate([qs_scr[crow(c), ksl(h)], sc_scr[c, h]], axis=1)
            rhs = jnp.concatenate([sb_scr[c, h], v_s[crow(c), vsl(h)]], axis=0)
            o = _dot(lhs, rhs)
            ms = jnp.mean(o * o, axis=-1, keepdims=True)
            og = o * lax.rsqrt(ms + EPS) * gn_ref[:, vsl(h)] * r_s[crow(c), vsl(h)].astype(F32)
            o_ref[crow(c), vsl(h)] = og.astype(BF16)
        if c % 2 == 1:
            fill_late(1)
    fill_late(len(late))

    @pl.when(j == pl.num_programs(1) - 1)
    def _():
        sf_ref[...] = s_scr[...]


def _mix(x2, g, ws, wup, bg, gn, s0, *, nb, rows_per_b, tm, n_pad):
    nc = tm // CHUNK
    nt = rows_per_b // tm
    rows = nb * rows_per_b
    row_spec = lambda w: pl.BlockSpec((tm, w), lambda b, i: (b * nt + i, 0))
    state_spec = pl.BlockSpec((None, HEADS, DK, DV), lambda b, i: (b, 0, 0, 0))
    s0_spec = pl.BlockSpec((None, HEADS, DK, DV), lambda b, i: (0, 0, 0, 0))
    assert s0.shape == (1, HEADS, DK, DV)
    in_specs = [row_spec(D_MODEL), _const_spec(g.shape)] + [_const_spec(w.shape) for w in ws]
    in_specs += [_const_spec(wup.shape), _const_spec(bg.shape), _const_spec(gn.shape), s0_spec]
    wide = jax.ShapeDtypeStruct((rows, D_MODEL), BF16)
    return pl.pallas_call(
        functools.partial(_mix_kernel, n_chunks=nc, n_pad=n_pad),
        grid=(nb, nt),
        in_specs=in_specs,
        out_specs=[row_spec(VAL), row_spec(WIDTH), row_spec(D_MODEL), row_spec(D_MODEL), state_spec],
        out_shape=[wide, wide, wide, wide, jax.ShapeDtypeStruct((nb, HEADS, DK, DV), F32)],
        scratch_shapes=[pltpu.VMEM((tm, KEY), BF16), pltpu.VMEM((tm, KEY), BF16),
                        pltpu.VMEM((tm, VAL), BF16), pltpu.VMEM((tm, VAL), BF16),
                        pltpu.VMEM((tm, KEY), F32),
                        pltpu.VMEM((HEADS, DK, DV), F32),
                        pltpu.VMEM((tm, KEY), F32),
                        pltpu.VMEM((tm, KEY), BF16), pltpu.VMEM((tm, KEY), BF16),
                        pltpu.VMEM((tm, KEY), BF16), pltpu.VMEM((tm, KEY), BF16),
                        pltpu.VMEM((nc, SUB, KEY), F32),
                        pltpu.VMEM((nc, HEADS, DK, DV), F32),
                        pltpu.VMEM((nc, HEADS, CHUNK, CHUNK), BF16),
                        pltpu.VMEM((nc, HEADS, DK, DV), BF16)],
        name="mix",
        compiler_params=pltpu.CompilerParams(
            dimension_semantics=("parallel", "arbitrary"), vmem_limit_bytes=VMEM_LIMIT),
    )(x2, g, *ws, wup, bg, gn, s0)


def _lambda_bar(ar, ai, ls):
    dt = jnp.exp(ls)
    mag = jnp.exp(ar * dt)
    return mag * jnp.cos(ai * dt), mag * jnp.sin(ai * dt)


def _s5disc_kernel(ars_ref, ais_ref, lss_ref, ar_ref, ai_ref, ls_ref, br_ref, bi_ref, cr_ref, ci_ref,
                   lre_o, lim_o, bblk_o, cblk_o):
    lre_s, lim_s = _lambda_bar(ars_ref[...], ais_ref[...], lss_ref[...])
    lre_o[...] = lre_s
    lim_o[...] = lim_s

    ar = ar_ref[...]
    ai = ai_ref[...]
    lre, lim = _lambda_bar(ar, ai, ls_ref[...])
    zr = lre - 1.0
    zi = lim
    den = ar * ar + ai * ai
    fr = (zr * ar + zi * ai) / den
    fi = (zi * ar - zr * ai) / den
    br = br_ref[...]
    bi = bi_ref[...]
    bb = (fr * br - fi * bi, fr * bi + fi * br)

    grp_of_row = lax.broadcasted_iota(jnp.int32, (SLAB, LANE), 0) // GSIZE
    half_of_lane = lax.broadcasted_iota(jnp.int32, (SLAB, LANE), 1) // NSTATE
    grp_of_col = lax.broadcasted_iota(jnp.int32, (LANE, SLAB), 1) // GSIZE
    half_of_row = lax.broadcasted_iota(jnp.int32, (LANE, SLAB), 0) // NSTATE
    for q in range(N_SLAB):
        ch = slice(q * SLAB, (q + 1) * SLAB)
        b_two = [jnp.concatenate([p[ch, :], p[ch, :]], axis=1) for p in bb]
        c_two = [jnp.concatenate([p[:, ch], p[:, ch]], axis=0)
                 for p in (cr_ref, ci_ref)]
        c_two[1] = -c_two[1]
        for j in range(N_TILE):
            own_b = grp_of_row == 2 * j + half_of_lane
            own_c = grp_of_col == 2 * j + half_of_row
            for p in range(2):
                cols = slice(j * TILE2 + p * LANE, j * TILE2 + (p + 1) * LANE)
                bblk_o[q, :, cols] = jnp.where(own_b, b_two[p], 0.0).astype(BF16)
                cblk_o[q, cols, :] = jnp.where(own_c, c_two[p], 0.0).astype(BF16)


def _s5disc(ars, ais, lss, ar, ai, ls, br, bi, cr, ci):
    lam = jax.ShapeDtypeStruct((N_SLAB, SLAB_STATE), F32)
    return pl.pallas_call(
        _s5disc_kernel,
        out_shape=[lam, lam, jax.ShapeDtypeStruct((N_SLAB, SLAB, 2 * SLAB_STATE), BF16),
                   jax.ShapeDtypeStruct((N_SLAB, 2 * SLAB_STATE, SLAB), BF16)],
        name="s5disc",
    )(ars, ais, lss, ar, ai, ls, br, bi, cr, ci)


S5_T = 64
S5_PAD = 4
SCAN_TILES = 4
SCAN_GROUP = 8


def _s5_kernel(u_ref, bblk_ref, cblk_ref, lre_ref, lim_ref, d_ref, h0_ref, y_ref, hf_ref,
               bu_scr, s_scr, acc_scr, stg_out, st_scr, *, T):
    i = pl.program_id(0)

    @pl.when(i == 0)
    def _():
        st_scr[...] = h0_ref[...]

    P = T + S5_PAD
    lane_tile = lambda lt: slice(lt * LANE, (lt + 1) * LANE)

    def in_piece(q, j):
        cols = slice(j * TILE2, (j + 1) * TILE2)
        res = _dot(u_ref[:, :, q * SLAB:(q + 1) * SLAB].reshape(SUB * T, SLAB), bblk_ref[q, :, cols])
        for part in range(2):
            for b in range(SUB):
                bu_scr[q % 2, 2 * j + part, b * P:b * P + T, :] = (
                    res[b * T:(b + 1) * T, part * LANE:(part + 1) * LANE])

    def out_piece(q, j):
        cols = slice(j * TILE2, (j + 1) * TILE2)
        part = _dot(s_scr[q % 2, :, cols], cblk_ref[q, cols, :])
        if j == 0:
            acc_scr[q % 2] = part
        else:
            acc_scr[q % 2] += part

    def scan_steps(q, tiles, lr, li, state, t0):
        for t in range(t0, t0 + SCAN_GROUP, 2):
            for j in tiles:
                re_c = slice(j * TILE2, j * TILE2 + LANE)
                im_c = slice(j * TILE2 + LANE, (j + 1) * TILE2)
                s_re, s_im = state[j]
                o_re, o_im = [], []
                for dt in range(2):
                    rows = pl.ds(t + dt, SUB, stride=P)
                    n_re = lr[j] * s_re - li[j] * s_im + bu_scr[q % 2, 2 * j, rows, :]
                    n_im = lr[j] * s_im + li[j] * s_re + bu_scr[q % 2, 2 * j + 1, rows, :]
                    s_re, s_im = n_re, n_im
                    o_re.append(n_re)
                    o_im.append(n_im)
                state[j] = (s_re, s_im)
                rows2 = slice(SUB * t, SUB * (t + 2))
                s_scr[q % 2, rows2, re_c] = jnp.concatenate(o_re, axis=0).astype(BF16)
                s_scr[q % 2, rows2, im_c] = jnp.concatenate(o_im, axis=0).astype(BF16)

    def regroup_out(q):
        for l2 in range(SLAB // LANE):
            lt = q * (SLAB // LANE) + l2
            for t in range(T):
                stg_out[lt, pl.ds(t, SUB, stride=P), :] = (
                    acc_scr[q % 2, SUB * t:SUB * (t + 1), lane_tile(l2)])
            for b in range(SUB):
                y = (stg_out[lt, b * P:b * P + T, :]
                     + d_ref[:, lane_tile(lt)] * u_ref[b, :, lane_tile(lt)].astype(F32))
                y_ref[b, :, lane_tile(lt)] = y.astype(BF16)

    groups_per_run = T // SCAN_GROUP
    n_groups = groups_per_run * (N_TILE // SCAN_TILES)
    for stage in range(N_SLAB + 2):
        q_in, q_scan, q_out = stage, stage - 1, stage - 2
        pieces = []
        for j in range(N_TILE):
            if q_in < N_SLAB:
                pieces.append(functools.partial(in_piece, q_in, j))
            if 0 <= q_out < N_SLAB:
                pieces.append(functools.partial(out_piece, q_out, j))
        scanning = 0 <= q_scan < N_SLAB
        for g in range(n_groups):
            for p in pieces[g * len(pieces) // n_groups:(g + 1) * len(pieces) // n_groups]:
                p()
            if not scanning:
                continue
            run, g_in_run = divmod(g, groups_per_run)
            tiles = range(run * SCAN_TILES, (run + 1) * SCAN_TILES)
            if g_in_run == 0:
                lr = {j: jnp.broadcast_to(lre_ref[q_scan:q_scan + 1, lane_tile(j)], (SUB, LANE))
                      for j in tiles}
                li = {j: jnp.broadcast_to(lim_ref[q_scan:q_scan + 1, lane_tile(j)], (SUB, LANE))
                      for j in tiles}
                state = {j: (st_scr[q_scan, :, j * TILE2:j * TILE2 + LANE],
                             st_scr[q_scan, :, j * TILE2 + LANE:(j + 1) * TILE2]) for j in tiles}
            scan_steps(q_scan, tiles, lr, li, state, g_in_run * SCAN_GROUP)
            if g_in_run == groups_per_run - 1:
                for j in tiles:
                    st_scr[q_scan, :, j * TILE2:j * TILE2 + LANE] = state[j][0]
                    st_scr[q_scan, :, j * TILE2 + LANE:(j + 1) * TILE2] = state[j][1]
        if 0 <= q_out < N_SLAB:
            regroup_out(q_out)

    @pl.when(i == pl.num_programs(0) - 1)
    def _():
        hf_ref[...] = st_scr[...]


def _s5(u3, bblk, cblk, lre, lim, d, h0, *, t_blk):
    nb, t_total, _ = u3.shape
    assert nb == SUB and t_total % t_blk == 0 and t_blk % (2 * SCAN_GROUP) == 0
    assert ((t_blk + S5_PAD) // 4) % 2 == 1
    blk = pl.BlockSpec((SUB, t_blk, WIDTH), lambda i: (0, i, 0))
    rows = SUB * t_blk
    n_lt = WIDTH // LANE
    pitch = t_blk + S5_PAD
    return pl.pallas_call(
        functools.partial(_s5_kernel, T=t_blk),
        grid=(t_total // t_blk,),
        in_specs=[blk, _const_spec(bblk.shape), _const_spec(cblk.shape),
                  _const_spec(lre.shape), _const_spec(lim.shape), _const_spec(d.shape),
                  _const_spec(h0.shape)],
        out_specs=[blk, _const_spec(h0.shape)],
        out_shape=[jax.ShapeDtypeStruct(u3.shape, BF16), jax.ShapeDtypeStruct(h0.shape, F32)],
        scratch_shapes=[pltpu.VMEM((2, 2 * N_TILE, SUB * pitch, LANE), F32),
                        pltpu.VMEM((2, rows, 2 * SLAB_STATE), BF16),
                        pltpu.VMEM((2, rows, SLAB), F32),
                        pltpu.VMEM((n_lt, SUB * pitch, LANE), F32),
                        pltpu.VMEM(h0.shape, F32)],
        name="s5",
        compiler_params=pltpu.CompilerParams(
            dimension_semantics=("arbitrary",), vmem_limit_bytes=VMEM_LIMIT),
    )(u3, bblk, cblk, lre, lim, d, h0)


FF_SPLIT = 4


POST_SPLIT = 2


def _post_kernel(x_ref, og_ref, ys_ref, sg_ref, ss_ref, wo, wglu, bglu, wout, gpost, gffn,
                 w1, w2, gfpost, out_ref):
    tm = x_ref.shape[0]
    sub = tm // POST_SPLIT
    rows = [slice(p * sub, (p + 1) * sub) for p in range(POST_SPLIT)]
    wcols = D_FF // FF_SPLIT
    st = [dict() for _ in rows]

    def s_gla(p, r):
        p["y_gla"] = _dot(og_ref[r, :], wo[...])

    def s_glu(p, r):
        ys = ys_ref[r, :].astype(F32)
        gs = 0.5 * ys * (1.0 + jnp.tanh(0.7978845608028654 * (ys + 0.044715 * (ys * ys * ys))))
        p["glu"] = _dot(gs.astype(BF16), wglu[...]) + bglu[...]

    def s_mix(p, r):
        glu = p.pop("glu")
        y_ssm = glu[:, :D_MODEL] * _sigmoid(glu[:, D_MODEL:])
        mixed = sg_ref[r, :].astype(F32) * p.pop("y_gla") + ss_ref[r, :].astype(F32) * y_ssm
        p["m2"] = _dot(mixed.astype(BF16), wout[...])

    def s_h1(p, r):
        p["h1"] = x_ref[r, :] + _rms(p.pop("m2"), gpost[...])
        p["hn"] = _rms(p["h1"], gffn[...]).astype(BF16)

    def s_ff(c):
        def run(p, r):
            a = jnp.maximum(_dot(p["hn"], w1[:, c * wcols:(c + 1) * wcols]), 0.0)
            part = _dot((a * a).astype(BF16), w2[c * wcols:(c + 1) * wcols, :])
            p["f"] = part if c == 0 else p["f"] + part
        return run

    def s_out(p, r):
        out_ref[r, :] = p["h1"] + _rms(p["f"], gfpost[...])

    for stage in [s_gla, s_glu, s_mix, s_h1] + [s_ff(c) for c in range(FF_SPLIT)] + [s_out]:
        for p, r in zip(st, rows):
            stage(p, r)


def _post(x2, og, ys, sg, ss, consts, *, nb, rows_per_b, tm):
    nt = rows_per_b // tm
    row_spec = lambda w: pl.BlockSpec((tm, w), lambda b, i: (b * nt + i, 0))
    in_specs = [row_spec(D_MODEL), row_spec(VAL), row_spec(WIDTH), row_spec(D_MODEL), row_spec(D_MODEL)]
    in_specs += [_const_spec(c.shape) for c in consts]
    return pl.pallas_call(
        _post_kernel, grid=(nb, nt), in_specs=in_specs, out_specs=row_spec(D_MODEL),
        out_shape=jax.ShapeDtypeStruct((nb * rows_per_b, D_MODEL), F32), name="post",
        compiler_params=pltpu.CompilerParams(
            dimension_semantics=("parallel", "parallel"), vmem_limit_bytes=VMEM_LIMIT),
    )(x2, og, ys, sg, ss, *consts)


def kernel(x, meta_tokens, g_mix_pre, w_in, w_gate_up, b_gate, gla_norm_g, w_o_gla, a_re, a_im,
           log_step, b_re, b_im, c_re, c_im, d_skip, w_glu, b_glu, w_out, g_mix_post, g_ffn_pre,
           w_ff1, w_ff2, g_ffn_post):
    bsz, seq, dm = x.shape
    assert dm == D_MODEL and bsz == SUB and seq % CHUNK == 0
    assert w_in.shape[0] == 1, "single layer"
    l = 0
    row = lambda t: t.reshape(1, -1).astype(F32)

    w = w_in[l]
    ws = (w[:, :W_HEAD].astype(BF16),
          jnp.pad(w[:, W_HEAD:W_HEAD + RANK], ((0, 0), (0, LANE - RANK))).astype(BF16),
          w[:, W_HEAD + RANK:].astype(BF16))
    wup = jnp.pad(w_gate_up[l], ((0, LANE - RANK), (0, 0))).astype(BF16)
    bg = row(b_gate[l])
    gpre = row(g_mix_pre[l])
    gn = row(gla_norm_g[l])

    rep = lambda t: jnp.repeat(t.astype(F32), GSIZE, axis=0)
    slabs = lambda t: t.astype(F32).reshape(N_SLAB, SLAB_STATE)
    rows_gh = lambda t: jnp.swapaxes(t, 1, 2).reshape(WIDTH, NSTATE).astype(F32)
    cols_gh = lambda t: jnp.transpose(t, (2, 0, 1)).reshape(NSTATE, WIDTH).astype(F32)
    ls_gn = jnp.broadcast_to(log_step[l][:, None], (GROUPS, NSTATE))
    lam_re, lam_im, bblk, cblk = _s5disc(
        slabs(a_re[l]), slabs(a_im[l]), slabs(ls_gn), rep(a_re[l]), rep(a_im[l]), rep(ls_gn),
        rows_gh(b_re[l]), rows_gh(b_im[l]), cols_gh(c_re[l]), cols_gh(c_im[l]))
    dsk = row(d_skip[l])

    meta_chunk = jnp.pad(meta_tokens.astype(F32), ((N_PAD, 0), (0, 0)))
    s_zero = jnp.zeros((1, HEADS, DK, DV), F32)
    _, mu, _, _, s_meta = _mix(meta_chunk, gpre, ws, wup, bg, gn, s_zero,
                               nb=1, rows_per_b=CHUNK, tm=CHUNK, n_pad=N_PAD)
    mu3 = jnp.broadcast_to(mu[None, N_PAD:], (SUB, N_META, WIDTH))
    h_zero = jnp.zeros((N_SLAB, SUB, 2 * SLAB_STATE), F32)
    _, h_meta = _s5(mu3, bblk, cblk, lam_re, lam_im, dsk, h_zero, t_blk=N_META)

    x2 = x.reshape(bsz * seq, dm)
    tm = 512 if seq % 512 == 0 else CHUNK
    og, u, sg, ss, _ = _mix(x2, gpre, ws, wup, bg, gn, s_meta, nb=bsz, rows_per_b=seq, tm=tm, n_pad=0)
    ys, _ = _s5(u.reshape(bsz, seq, WIDTH), bblk, cblk, lam_re, lam_im, dsk, h_meta, t_blk=S5_T)
    consts = (w_o_gla[l].astype(BF16), w_glu[l].astype(BF16), row(b_glu[l]), w_out[l].astype(BF16),
              row(g_mix_post[l]), row(g_ffn_pre[l]), w_ff1[l].astype(BF16), w_ff2[l].astype(BF16),
              row(g_ffn_post[l]))
    out = _post(x2, og, ys.reshape(bsz * seq, WIDTH), sg, ss, consts, nb=bsz, rows_per_b=seq, tm=tm)
    return out.reshape(bsz, seq, dm)
```

```python
import functools

import jax
import jax.numpy as jnp
from jax import lax
from jax.experimental import pallas as pl
from jax.experimental.pallas import tpu as pltpu

F32 = jnp.float32
BF16 = jnp.bfloat16

D_MODEL = 1024
N_META = 16
CHUNK = 64
N_PAD = CHUNK - N_META
HEADS = 4
DK = 128
DV = 256
KEY = HEADS * DK
VAL = HEADS * DV
RANK = 16
TAU = 16.0
GROUPS = 64
GSIZE = 16
NSTATE = 64
WIDTH = GROUPS * GSIZE
D_FF = 4 * D_MODEL
EPS = 1e-6

LANE = 128
SUB = 8
SLAB = 256
SLAB_GROUPS = SLAB // GSIZE
N_SLAB = WIDTH // SLAB
SLAB_STATE = SLAB_GROUPS * NSTATE
N_TILE = SLAB_STATE // LANE
TILE2 = 2 * LANE
VMEM_LIMIT = 56 * 1024 * 1024


def _const_spec(shape):
    nd = len(shape)
    return pl.BlockSpec(shape, lambda *_: (0,) * nd, pipeline_mode=pl.Buffered(1))


def _rms(x, g):
    ms = jnp.mean(x * x, axis=-1, keepdims=True)
    return x * lax.rsqrt(ms + EPS) * g


def _sigmoid(x):
    return 1.0 / (1.0 + jnp.exp(-x))


def _dot(a, b):
    return jnp.dot(a, b, preferred_element_type=F32)


PROJ_SPLIT = 2
W_COLS = {"q": (0, 0, KEY), "k": (0, KEY, KEY), "v": (0, 2 * KEY, VAL), "r": (0, 2 * KEY + VAL, VAL),
          "a": (1, 0, LANE),
          "u": (2, 0, WIDTH), "zg": (2, WIDTH, D_MODEL), "zs": (2, WIDTH + D_MODEL, D_MODEL)}
W_HEAD = 2 * KEY + 2 * VAL


def _mix_kernel(x_ref, g_ref, w_head, w_a, w_tail, wup, bg, gn_ref, s0_ref,
                o_ref, u_o, zg_o, zs_o, sf_ref,
                q_s, k_s, v_s, r_s, la_s,
                s_scr, b_scr, qi_scr, ki_scr, qs_scr, ks_scr, dc_scr, kv_scr, sc_scr, sb_scr,
                *, n_chunks, n_pad):
    j = pl.program_id(1)

    @pl.when(j == 0)
    def _():
        s_scr[...] = s0_ref[...]

    xn = _rms(x_ref[...], g_ref[...]).astype(BF16)

    def mm(name, lo=0, hi=None):
        which, start, width = W_COLS[name]
        hi = width if hi is None else hi
        return _dot(xn, (w_head, w_a, w_tail)[which][:, start + lo:start + hi])

    def proj_q():
        q_s[...] = (mm("q") * (DK ** -0.5)).astype(BF16)

    def proj_k():
        k_s[...] = mm("k").astype(BF16)

    def proj_v():
        v_s[...] = mm("v").astype(BF16)

    def proj_r():
        r = mm("r")
        r_s[...] = (r * _sigmoid(r)).astype(BF16)

    def proj_piece(name, o, act, c):
        lo, hi = c * (D_MODEL // PROJ_SPLIT), (c + 1) * (D_MODEL // PROJ_SPLIT)
        o[:, lo:hi] = act(mm(name, lo, hi)).astype(BF16)

    late = [functools.partial(proj_piece, name, o, act, c)
            for name, o, act in (("u", u_o, lambda t: t), ("zg", zg_o, _sigmoid), ("zs", zs_o, _sigmoid))
            for c in range(PROJ_SPLIT)]

    def fill_late(n):
        for _ in range(min(n, len(late))):
            late.pop(0)()

    z = _dot(mm("a").astype(BF16), wup[...]) + bg[...]
    la_s[...] = (jnp.minimum(z, 0.0) - jnp.log(1.0 + jnp.exp(-jnp.abs(z)))) * (1.0 / TAU)
    if n_pad:
        la_s[0:n_pad, :] = jnp.zeros((n_pad, KEY), F32)

    row = lax.broadcasted_iota(jnp.int32, (CHUNK, CHUNK), 0)
    col = lax.broadcasted_iota(jnp.int32, (CHUNK, CHUNK), 1)
    causal = row >= col
    tril = jnp.where(causal, 1.0, 0.0).astype(BF16)
    tn = (((0,), (0,)), ((), ()))
    nt = (((1,), (1,)), ((), ()))
    crow = lambda c: slice(c * CHUNK, (c + 1) * CHUNK)
    ksl = lambda h: slice(h * DK, (h + 1) * DK)
    vsl = lambda h: slice(h * DV, (h + 1) * DV)

    proj_q()
    for c in range(n_chunks):
        la = la_s[crow(c), :]
        la_hi = la.astype(BF16)
        la_lo = (la - la_hi.astype(F32)).astype(BF16)
        b_scr[crow(c), :] = _dot(tril, la_hi) + _dot(tril, la_lo)

    proj_k()
    for c in range(n_chunks):
        b = b_scr[crow(c), :]
        b_mid = b[CHUNK // 2:CHUNK // 2 + 1, :]
        b_last = b[CHUNK - 1:CHUNK, :]
        q_in = q_s[crow(c), :].astype(F32) * jnp.exp(b - b_mid)
        k_in = k_s[crow(c), :].astype(F32) * jnp.exp(b_mid - b)
        qs_scr[crow(c), :] = (q_in * jnp.exp(b_mid)).astype(BF16)
        ks_scr[crow(c), :] = (k_in * jnp.exp(b_last - b_mid)).astype(BF16)
        qi_scr[crow(c), :] = q_in.astype(BF16)
        ki_scr[crow(c), :] = k_in.astype(BF16)
        dc_scr[c] = jnp.broadcast_to(jnp.exp(b_last), (SUB, KEY))

    proj_v()
    for c in range(n_chunks):
        for h in range(HEADS):
            sc = lax.dot_general(qi_scr[crow(c), ksl(h)], ki_scr[crow(c), ksl(h)], nt,
                                 preferred_element_type=F32)
            sc_scr[c, h] = jnp.where(causal, sc, 0.0).astype(BF16)

    proj_r()
    for c in range(n_chunks):
        for h in range(HEADS):
            kv_scr[c, h] = lax.dot_general(ks_scr[crow(c), ksl(h)], v_s[crow(c), vsl(h)], tn,
                                           preferred_element_type=F32)

    fill_late(1)
    for c in range(n_chunks):
        for h in range(HEADS):
            s_old = s_scr[h]
            sb_scr[c, h] = s_old.astype(BF16)
            dcb = jnp.broadcast_to(dc_scr[c, 0:1, ksl(h)], (DK, DK)).T
            s_scr[h] = jnp.concatenate([dcb, dcb], axis=1) * s_old + kv_scr[c, h]

    fill_late(1)
    for c in range(n_chunks):
        for h in range(HEADS):
            lhs = jnp.concatenate([qs_scr[crow(c), ksl(h)], sc_scr[c, h]], axis=1)
            rhs = jnp.concatenate([sb_scr[c, h], v_s[crow(c), vsl(h)]], axis=0)
            o = _dot(lhs, rhs)
            ms = jnp.mean(o * o, axis=-1, keepdims=True)
            og = o * lax.rsqrt(ms + EPS) * gn_ref[:, vsl(h)] * r_s[crow(c), vsl(h)].astype(F32)
            o_ref[crow(c), vsl(h)] = og.astype(BF16)
        if c % 2 == 1:
            fill_late(1)
    fill_late(len(late))

    @pl.when(j == pl.num_programs(1) - 1)
    def _():
        sf_ref[...] = s_scr[...]


def _mix(x2, g, ws, wup, bg, gn, s0, *, nb, rows_per_b, tm, n_pad):
    nc = tm // CHUNK
    nt = rows_per_b // tm
    rows = nb * rows_per_b
    row_spec = lambda w: pl.BlockSpec((tm, w), lambda b, i: (b * nt + i, 0))
    state_spec = pl.BlockSpec((None, HEADS, DK, DV), lambda b, i: (b, 0, 0, 0))
    s0_spec = pl.BlockSpec((None, HEADS, DK, DV), lambda b, i: (0, 0, 0, 0))
    assert s0.shape == (1, HEADS, DK, DV)
    in_specs = [row_spec(D_MODEL), _const_spec(g.shape)] + [_const_spec(w.shape) for w in ws]
    in_specs += [_const_spec(wup.shape), _const_spec(bg.shape), _const_spec(gn.shape), s0_spec]
    wide = jax.ShapeDtypeStruct((rows, D_MODEL), BF16)
    return pl.pallas_call(
        functools.partial(_mix_kernel, n_chunks=nc, n_pad=n_pad),
        grid=(nb, nt),
        in_specs=in_specs,
        out_specs=[row_spec(VAL), row_spec(WIDTH), row_spec(D_MODEL), row_spec(D_MODEL), state_spec],
        out_shape=[wide, wide, wide, wide, jax.ShapeDtypeStruct((nb, HEADS, DK, DV), F32)],
        scratch_shapes=[pltpu.VMEM((tm, KEY), BF16), pltpu.VMEM((tm, KEY), BF16),
                        pltpu.VMEM((tm, VAL), BF16), pltpu.VMEM((tm, VAL), BF16),
                        pltpu.VMEM((tm, KEY), F32),
                        pltpu.VMEM((HEADS, DK, DV), F32),
                        pltpu.VMEM((tm, KEY), F32),
                        pltpu.VMEM((tm, KEY), BF16), pltpu.VMEM((tm, KEY), BF16),
                        pltpu.VMEM((tm, KEY), BF16), pltpu.VMEM((tm, KEY), BF16),
                        pltpu.VMEM((nc, SUB, KEY), F32),
                        pltpu.VMEM((nc, HEADS, DK, DV), F32),
                        pltpu.VMEM((nc, HEADS, CHUNK, CHUNK), BF16),
                        pltpu.VMEM((nc, HEADS, DK, DV), BF16)],
        name="mix",
        compiler_params=pltpu.CompilerParams(
            dimension_semantics=("parallel", "arbitrary"), vmem_limit_bytes=VMEM_LIMIT),
    )(x2, g, *ws, wup, bg, gn, s0)


def _lambda_bar(ar, ai, ls):
    dt = jnp.exp(ls)
    mag = jnp.exp(ar * dt)
    return mag * jnp.cos(ai * dt), mag * jnp.sin(ai * dt)


def _s5disc_kernel(ars_ref, ais_ref, lss_ref, ar_ref, ai_ref, ls_ref, br_ref, bi_ref, cr_ref, ci_ref,
                   lre_o, lim_o, bblk_o, cblk_o):
    lre_s, lim_s = _lambda_bar(ars_ref[...], ais_ref[...], lss_ref[...])
    lre_o[...] = lre_s
    lim_o[...] = lim_s

    ar = ar_ref[...]
    ai = ai_ref[...]
    lre, lim = _lambda_bar(ar, ai, ls_ref[...])
    zr = lre - 1.0
    zi = lim
    den = ar * ar + ai * ai
    fr = (zr * ar + zi * ai) / den
    fi = (zi * ar - zr * ai) / den
    br = br_ref[...]
    bi = bi_ref[...]
    bb = (fr * br - fi * bi, fr * bi + fi * br)

    grp_of_row = lax.broadcasted_iota(jnp.int32, (SLAB, LANE), 0) // GSIZE
    half_of_lane = lax.broadcasted_iota(jnp.int32, (SLAB, LANE), 1) // NSTATE
    grp_of_col = lax.broadcasted_iota(jnp.int32, (LANE, SLAB), 1) // GSIZE
    half_of_row = lax.broadcasted_iota(jnp.int32, (LANE, SLAB), 0) // NSTATE
    for q in range(N_SLAB):
        ch = slice(q * SLAB, (q + 1) * SLAB)
        b_two = [jnp.concatenate([p[ch, :], p[ch, :]], axis=1) for p in bb]
        c_two = [jnp.concatenate([p[:, ch], p[:, ch]], axis=0)
                 for p in (cr_ref, ci_ref)]
        c_two[1] = -c_two[1]
        for j in range(N_TILE):
            own_b = grp_of_row == 2 * j + half_of_lane
            own_c = grp_of_col == 2 * j + half_of_row
            for p in range(2):
                cols = slice(j * TILE2 + p * LANE, j * TILE2 + (p + 1) * LANE)
                bblk_o[q, :, cols] = jnp.where(own_b, b_two[p], 0.0).astype(BF16)
                cblk_o[q, cols, :] = jnp.where(own_c, c_two[p], 0.0).astype(BF16)


def _s5disc(ars, ais, lss, ar, ai, ls, br, bi, cr, ci):
    lam = jax.ShapeDtypeStruct((N_SLAB, SLAB_STATE), F32)
    return pl.pallas_call(
        _s5disc_kernel,
        out_shape=[lam, lam, jax.ShapeDtypeStruct((N_SLAB, SLAB, 2 * SLAB_STATE), BF16),
                   jax.ShapeDtypeStruct((N_SLAB, 2 * SLAB_STATE, SLAB), BF16)],
        name="s5disc",
    )(ars, ais, lss, ar, ai, ls, br, bi, cr, ci)


S5_T = 64
S5_PAD = 4
SCAN_TILES = 4
SCAN_GROUP = 8


def _s5_kernel(u_ref, bblk_ref, cblk_ref, lre_ref, lim_ref, d_ref, h0_ref, *rest, T, n_cast):
    cast_in, rest = rest[:n_cast], rest[n_cast:]
    y_ref, hf_ref = rest[:2]
    cast_out, (bu_scr, s_scr, acc_scr, stg_out, st_scr) = rest[2:2 + n_cast], rest[2 + n_cast:]
    i = pl.program_id(0)
    for w_f32, w_bf16 in zip(cast_in, cast_out):
        w_bf16[...] = w_f32[...].astype(BF16)

    @pl.when(i == 0)
    def _():
        st_scr[...] = h0_ref[...]

    P = T + S5_PAD
    lane_tile = lambda lt: slice(lt * LANE, (lt + 1) * LANE)

    def in_piece(q, j):
        cols = slice(j * TILE2, (j + 1) * TILE2)
        res = _dot(u_ref[:, :, q * SLAB:(q + 1) * SLAB].reshape(SUB * T, SLAB), bblk_ref[q, :, cols])
        for part in range(2):
            for b in range(SUB):
                bu_scr[q % 2, 2 * j + part, b * P:b * P + T, :] = (
                    res[b * T:(b + 1) * T, part * LANE:(part + 1) * LANE])

    def out_piece(q, j):
        cols = slice(j * TILE2, (j + 1) * TILE2)
        part = _dot(s_scr[q % 2, :, cols], cblk_ref[q, cols, :])
        if j == 0:
            acc_scr[q % 2] = part
        else:
            acc_scr[q % 2] += part

    def scan_steps(q, tiles, lr, li, state, t0):
        for t in range(t0, t0 + SCAN_GROUP, 2):
            for j in tiles:
                re_c = slice(j * TILE2, j * TILE2 + LANE)
                im_c = slice(j * TILE2 + LANE, (j + 1) * TILE2)
                s_re, s_im = state[j]
                o_re, o_im = [], []
                for dt in range(2):
                    rows = pl.ds(t + dt, SUB, stride=P)
                    n_re = lr[j] * s_re - li[j] * s_im + bu_scr[q % 2, 2 * j, rows, :]
                    n_im = lr[j] * s_im + li[j] * s_re + bu_scr[q % 2, 2 * j + 1, rows, :]
                    s_re, s_im = n_re, n_im
                    o_re.append(n_re)
                    o_im.append(n_im)
                state[j] = (s_re, s_im)
                rows2 = slice(SUB * t, SUB * (t + 2))
                s_scr[q % 2, rows2, re_c] = jnp.concatenate(o_re, axis=0).astype(BF16)
                s_scr[q % 2, rows2, im_c] = jnp.concatenate(o_im, axis=0).astype(BF16)

    def regroup_out(q):
        for l2 in range(SLAB // LANE):
            lt = q * (SLAB // LANE) + l2
            for t in range(T):
                stg_out[lt, pl.ds(t, SUB, stride=P), :] = (
                    acc_scr[q % 2, SUB * t:SUB * (t + 1), lane_tile(l2)])
            for b in range(SUB):
                y = (stg_out[lt, b * P:b * P + T, :]
                     + d_ref[:, lane_tile(lt)] * u_ref[b, :, lane_tile(lt)].astype(F32))
                y_ref[b, :, lane_tile(lt)] = y.astype(BF16)

    groups_per_run = T // SCAN_GROUP
    n_groups = groups_per_run * (N_TILE // SCAN_TILES)
    for stage in range(N_SLAB + 2):
        q_in, q_scan, q_out = stage, stage - 1, stage - 2
        pieces = []
        for j in range(N_TILE):
            if q_in < N_SLAB:
                pieces.append(functools.partial(in_piece, q_in, j))
            if 0 <= q_out < N_SLAB:
                pieces.append(functools.partial(out_piece, q_out, j))
        scanning = 0 <= q_scan < N_SLAB
        for g in range(n_groups):
            for p in pieces[g * len(pieces) // n_groups:(g + 1) * len(pieces) // n_groups]:
                p()
            if not scanning:
                continue
            run, g_in_run = divmod(g, groups_per_run)
            tiles = range(run * SCAN_TILES, (run + 1) * SCAN_TILES)
            if g_in_run == 0:
                lr = {j: jnp.broadcast_to(lre_ref[q_scan:q_scan + 1, lane_tile(j)], (SUB, LANE))
                      for j in tiles}
                li = {j: jnp.broadcast_to(lim_ref[q_scan:q_scan + 1, lane_tile(j)], (SUB, LANE))
                      for j in tiles}
                state = {j: (st_scr[q_scan, :, j * TILE2:j * TILE2 + LANE],
                             st_scr[q_scan, :, j * TILE2 + LANE:(j + 1) * TILE2]) for j in tiles}
            scan_steps(q_scan, tiles, lr, li, state, g_in_run * SCAN_GROUP)
            if g_in_run == groups_per_run - 1:
                for j in tiles:
                    st_scr[q_scan, :, j * TILE2:j * TILE2 + LANE] = state[j][0]
                    st_scr[q_scan, :, j * TILE2 + LANE:(j + 1) * TILE2] = state[j][1]
        if 0 <= q_out < N_SLAB:
            regroup_out(q_out)

    @pl.when(i == pl.num_programs(0) - 1)
    def _():
        hf_ref[...] = st_scr[...]


def _s5(u3, bblk, cblk, lre, lim, d, h0, *, t_blk, to_bf16=()):
    nb, t_total, _ = u3.shape
    assert nb == SUB and t_total % t_blk == 0 and t_blk % (2 * SCAN_GROUP) == 0
    assert ((t_blk + S5_PAD) // 4) % 2 == 1
    n_steps = t_total // t_blk
    blk = pl.BlockSpec((SUB, t_blk, WIDTH), lambda i: (0, i, 0))
    rows = SUB * t_blk
    n_lt = WIDTH // LANE
    pitch = t_blk + S5_PAD
    assert all(w.shape[0] % (2 * SUB * n_steps) == 0 for w in to_bf16)
    cast_specs = [pl.BlockSpec((w.shape[0] // n_steps, w.shape[1]), lambda i: (i, 0)) for w in to_bf16]
    return pl.pallas_call(
        functools.partial(_s5_kernel, T=t_blk, n_cast=len(to_bf16)),
        grid=(n_steps,),
        in_specs=[blk, _const_spec(bblk.shape), _const_spec(cblk.shape),
                  _const_spec(lre.shape), _const_spec(lim.shape), _const_spec(d.shape),
                  _const_spec(h0.shape)] + cast_specs,
        out_specs=[blk, _const_spec(h0.shape)] + cast_specs,
        out_shape=[jax.ShapeDtypeStruct(u3.shape, BF16), jax.ShapeDtypeStruct(h0.shape, F32)]
        + [jax.ShapeDtypeStruct(w.shape, BF16) for w in to_bf16],
        scratch_shapes=[pltpu.VMEM((2, 2 * N_TILE, SUB * pitch, LANE), F32),
                        pltpu.VMEM((2, rows, 2 * SLAB_STATE), BF16),
                        pltpu.VMEM((2, rows, SLAB), F32),
                        pltpu.VMEM((n_lt, SUB * pitch, LANE), F32),
                        pltpu.VMEM(h0.shape, F32)],
        name="s5",
        compiler_params=pltpu.CompilerParams(
            dimension_semantics=("arbitrary",), vmem_limit_bytes=VMEM_LIMIT),
    )(u3, bblk, cblk, lre, lim, d, h0, *to_bf16)


FF_SPLIT = 4


POST_SPLIT = 2


def _post_kernel(x_ref, og_ref, ys_ref, sg_ref, ss_ref, wo, wglu, bglu, wout, gpost, gffn,
                 w1, w2, gfpost, out_ref):
    tm = x_ref.shape[0]
    sub = tm // POST_SPLIT
    rows = [slice(p * sub, (p + 1) * sub) for p in range(POST_SPLIT)]
    wcols = D_FF // FF_SPLIT
    st = [dict() for _ in rows]

    def s_gla(p, r):
        p["y_gla"] = _dot(og_ref[r, :], wo[...])

    def s_glu(p, r):
        ys = ys_ref[r, :].astype(F32)
        gs = 0.5 * ys * (1.0 + jnp.tanh(0.7978845608028654 * (ys + 0.044715 * (ys * ys * ys))))
        p["glu"] = _dot(gs.astype(BF16), wglu[...]) + bglu[...]

    def s_mix(p, r):
        glu = p.pop("glu")
        y_ssm = glu[:, :D_MODEL] * _sigmoid(glu[:, D_MODEL:])
        mixed = sg_ref[r, :].astype(F32) * p.pop("y_gla") + ss_ref[r, :].astype(F32) * y_ssm
        p["m2"] = _dot(mixed.astype(BF16), wout[...])

    def s_h1(p, r):
        p["h1"] = x_ref[r, :] + _rms(p.pop("m2"), gpost[...])
        p["hn"] = _rms(p["h1"], gffn[...]).astype(BF16)

    def s_ff(c):
        def run(p, r):
            a = jnp.maximum(_dot(p["hn"], w1[:, c * wcols:(c + 1) * wcols]), 0.0)
            part = _dot((a * a).astype(BF16), w2[c * wcols:(c + 1) * wcols, :])
            p["f"] = part if c == 0 else p["f"] + part
        return run

    def s_out(p, r):
        out_ref[r, :] = p["h1"] + _rms(p["f"], gfpost[...])

    for stage in [s_gla, s_glu, s_mix, s_h1] + [s_ff(c) for c in range(FF_SPLIT)] + [s_out]:
        for p, r in zip(st, rows):
            stage(p, r)


def _post(x2, og, ys, sg, ss, consts, *, nb, rows_per_b, tm):
    nt = rows_per_b // tm
    row_spec = lambda w: pl.BlockSpec((tm, w), lambda b, i: (b * nt + i, 0))
    in_specs = [row_spec(D_MODEL), row_spec(VAL), row_spec(WIDTH), row_spec(D_MODEL), row_spec(D_MODEL)]
    in_specs += [_const_spec(c.shape) for c in consts]
    return pl.pallas_call(
        _post_kernel, grid=(nb, nt), in_specs=in_specs, out_specs=row_spec(D_MODEL),
        out_shape=jax.ShapeDtypeStruct((nb * rows_per_b, D_MODEL), F32), name="post",
        compiler_params=pltpu.CompilerParams(
            dimension_semantics=("parallel", "parallel"), vmem_limit_bytes=VMEM_LIMIT),
    )(x2, og, ys, sg, ss, *consts)


def kernel(x, meta_tokens, g_mix_pre, w_in, w_gate_up, b_gate, gla_norm_g, w_o_gla, a_re, a_im,
           log_step, b_re, b_im, c_re, c_im, d_skip, w_glu, b_glu, w_out, g_mix_post, g_ffn_pre,
           w_ff1, w_ff2, g_ffn_post):
    bsz, seq, dm = x.shape
    assert dm == D_MODEL and bsz == SUB and seq % CHUNK == 0
    assert w_in.shape[0] == 1, "single layer"
    l = 0
    row = lambda t: t.reshape(1, -1).astype(F32)

    w = w_in[l]
    ws = (w[:, :W_HEAD].astype(BF16),
          jnp.pad(w[:, W_HEAD:W_HEAD + RANK], ((0, 0), (0, LANE - RANK))).astype(BF16),
          w[:, W_HEAD + RANK:].astype(BF16))
    wup = jnp.pad(w_gate_up[l], ((0, LANE - RANK), (0, 0))).astype(BF16)
    bg = row(b_gate[l])
    gpre = row(g_mix_pre[l])
    gn = row(gla_norm_g[l])

    rep = lambda t: jnp.repeat(t.astype(F32), GSIZE, axis=0)
    slabs = lambda t: t.astype(F32).reshape(N_SLAB, SLAB_STATE)
    rows_gh = lambda t: jnp.swapaxes(t, 1, 2).reshape(WIDTH, NSTATE).astype(F32)
    cols_gh = lambda t: jnp.transpose(t, (2, 0, 1)).reshape(NSTATE, WIDTH).astype(F32)
    ls_gn = jnp.broadcast_to(log_step[l][:, None], (GROUPS, NSTATE))
    lam_re, lam_im, bblk, cblk = _s5disc(
        slabs(a_re[l]), slabs(a_im[l]), slabs(ls_gn), rep(a_re[l]), rep(a_im[l]), rep(ls_gn),
        rows_gh(b_re[l]), rows_gh(b_im[l]), cols_gh(c_re[l]), cols_gh(c_im[l]))
    dsk = row(d_skip[l])

    meta_chunk = jnp.pad(meta_tokens.astype(F32), ((N_PAD, 0), (0, 0)))
    s_zero = jnp.zeros((1, HEADS, DK, DV), F32)
    _, mu, _, _, s_meta = _mix(meta_chunk, gpre, ws, wup, bg, gn, s_zero,
                               nb=1, rows_per_b=CHUNK, tm=CHUNK, n_pad=N_PAD)
    mu3 = jnp.broadcast_to(mu[None, N_PAD:], (SUB, N_META, WIDTH))
    h_zero = jnp.zeros((N_SLAB, SUB, 2 * SLAB_STATE), F32)
    _, h_meta = _s5(mu3, bblk, cblk, lam_re, lam_im, dsk, h_zero, t_blk=N_META)[:2]

    x2 = x.reshape(bsz * seq, dm)
    tm = 512 if seq % 512 == 0 else CHUNK
    og, u, sg, ss, _ = _mix(x2, gpre, ws, wup, bg, gn, s_meta, nb=bsz, rows_per_b=seq, tm=tm, n_pad=0)
    ys, _, wo_b, wglu_b, wout_b, w1_b, w2_b = _s5(
        u.reshape(bsz, seq, WIDTH), bblk, cblk, lam_re, lam_im, dsk, h_meta, t_blk=S5_T,
        to_bf16=(w_o_gla[l], w_glu[l], w_out[l], w_ff1[l], w_ff2[l]))
    consts = (wo_b, wglu_b, row(b_glu[l]), wout_b, row(g_mix_post[l]), row(g_ffn_pre[l]), w1_b, w2_b,
              row(g_ffn_post[l]))
    out = _post(x2, og, ys.reshape(bsz * seq, WIDTH), sg, ss, consts, nb=bsz, rows_per_b=seq, tm=tm)
    return out.reshape(bsz, seq, dm)
```

```python
import functools

import jax
import jax.numpy as jnp
from jax import lax
from jax.experimental import pallas as pl
from jax.experimental.pallas import tpu as pltpu

F32 = jnp.float32
BF16 = jnp.bfloat16

D_MODEL = 1024
N_META = 16
CHUNK = 64
N_PAD = CHUNK - N_META
HEADS = 4
DK = 128
DV = 256
KEY = HEADS * DK
VAL = HEADS * DV
RANK = 16
TAU = 16.0
GROUPS = 64
GSIZE = 16
NSTATE = 64
WIDTH = GROUPS * GSIZE
D_FF = 4 * D_MODEL
EPS = 1e-6

LANE = 128
SUB = 8
SLAB = 256
SLAB_GROUPS = SLAB // GSIZE
N_SLAB = WIDTH // SLAB
SLAB_STATE = SLAB_GROUPS * NSTATE
N_TILE = SLAB_STATE // LANE
TILE2 = 2 * LANE
VMEM_LIMIT = 56 * 1024 * 1024


def _const_spec(shape):
    nd = len(shape)
    return pl.BlockSpec(shape, lambda *_: (0,) * nd, pipeline_mode=pl.Buffered(1))


def _rms(x, g):
    ms = jnp.mean(x * x, axis=-1, keepdims=True)
    return x * lax.rsqrt(ms + EPS) * g


def _sigmoid(x):
    return 0.5 * jnp.tanh(0.5 * x) + 0.5


def _dot(a, b):
    return jnp.dot(a, b, preferred_element_type=F32)


PROJ_SPLIT = 2
W_COLS = {"q": (0, 0, KEY), "k": (0, KEY, KEY), "v": (0, 2 * KEY, VAL), "r": (0, 2 * KEY + VAL, VAL),
          "a": (1, 0, LANE),
          "u": (2, 0, WIDTH), "zg": (2, WIDTH, D_MODEL), "zs": (2, WIDTH + D_MODEL, D_MODEL)}
W_HEAD = 2 * KEY + 2 * VAL


def _mix_kernel(x_ref, g_ref, w_head, w_a, w_tail, wup, bg, gn_ref, s0_ref,
                o_ref, u_o, zg_o, zs_o, sf_ref,
                q_s, k_s, v_s, r_s, la_s,
                s_scr, b_scr, qi_scr, ki_scr, qs_scr, ks_scr, dc_scr, kv_scr, sc_scr, sb_scr,
                *, n_chunks, n_pad):
    j = pl.program_id(1)

    @pl.when(j == 0)
    def _():
        s_scr[...] = s0_ref[...]

    xn = _rms(x_ref[...], g_ref[...]).astype(BF16)

    def mm(name, lo=0, hi=None):
        which, start, width = W_COLS[name]
        hi = width if hi is None else hi
        return _dot(xn, (w_head, w_a, w_tail)[which][:, start + lo:start + hi])

    def proj_q():
        q_s[...] = (mm("q") * (DK ** -0.5)).astype(BF16)

    def proj_k():
        k_s[...] = mm("k").astype(BF16)

    def proj_v():
        v_s[...] = mm("v").astype(BF16)

    def proj_r():
        r = mm("r")
        r_s[...] = (r * _sigmoid(r)).astype(BF16)

    def proj_piece(name, o, act, c):
        lo, hi = c * (D_MODEL // PROJ_SPLIT), (c + 1) * (D_MODEL // PROJ_SPLIT)
        o[:, lo:hi] = act(mm(name, lo, hi)).astype(BF16)

    late = [functools.partial(proj_piece, name, o, act, c)
            for name, o, act in (("u", u_o, lambda t: t), ("zg", zg_o, _sigmoid), ("zs", zs_o, _sigmoid))
            for c in range(PROJ_SPLIT)]

    def fill_late(n):
        for _ in range(min(n, len(late))):
            late.pop(0)()

    z = _dot(mm("a").astype(BF16), wup[...]) + bg[...]
    la_s[...] = (jnp.minimum(z, 0.0) - jnp.log(1.0 + jnp.exp(-jnp.abs(z)))) * (1.0 / TAU)
    if n_pad:
        la_s[0:n_pad, :] = jnp.zeros((n_pad, KEY), F32)

    row = lax.broadcasted_iota(jnp.int32, (CHUNK, CHUNK), 0)
    col = lax.broadcasted_iota(jnp.int32, (CHUNK, CHUNK), 1)
    causal = row >= col
    tril = jnp.where(causal, 1.0, 0.0).astype(BF16)
    tn = (((0,), (0,)), ((), ()))
    nt = (((1,), (1,)), ((), ()))
    crow = lambda c: slice(c * CHUNK, (c + 1) * CHUNK)
    ksl = lambda h: slice(h * DK, (h + 1) * DK)
    vsl = lambda h: slice(h * DV, (h + 1) * DV)

    proj_q()
    for c in range(n_chunks):
        la = la_s[crow(c), :]
        la_hi = la.astype(BF16)
        la_lo = (la - la_hi.astype(F32)).astype(BF16)
        b_scr[crow(c), :] = _dot(tril, la_hi) + _dot(tril, la_lo)

    proj_k()
    for c in range(n_chunks):
        b = b_scr[crow(c), :]
        b_mid = b[CHUNK // 2:CHUNK // 2 + 1, :]
        b_last = b[CHUNK - 1:CHUNK, :]
        q_in = q_s[crow(c), :].astype(F32) * jnp.exp(b - b_mid)
        k_in = k_s[crow(c), :].astype(F32) * jnp.exp(b_mid - b)
        qs_scr[crow(c), :] = (q_in * jnp.exp(b_mid)).astype(BF16)
        ks_scr[crow(c), :] = (k_in * jnp.exp(b_last - b_mid)).astype(BF16)
        qi_scr[crow(c), :] = q_in.astype(BF16)
        ki_scr[crow(c), :] = k_in.astype(BF16)
        dc_scr[c] = jnp.broadcast_to(jnp.exp(b_last), (SUB, KEY))

    proj_v()
    for c in range(n_chunks):
        for h in range(HEADS):
            sc = lax.dot_general(qi_scr[crow(c), ksl(h)], ki_scr[crow(c), ksl(h)], nt,
                                 preferred_element_type=F32)
            sc_scr[c, h] = jnp.where(causal, sc, 0.0).astype(BF16)

    proj_r()
    for c in range(n_chunks):
        for h in range(HEADS):
            kv_scr[c, h] = lax.dot_general(ks_scr[crow(c), ksl(h)], v_s[crow(c), vsl(h)], tn,
                                           preferred_element_type=F32)

    fill_late(1)
    for c in range(n_chunks):
        for h in range(HEADS):
            s_old = s_scr[h]
            sb_scr[c, h] = s_old.astype(BF16)
            dcb = jnp.broadcast_to(dc_scr[c, 0:1, ksl(h)], (DK, DK)).T
            s_scr[h] = jnp.concatenate([dcb, dcb], axis=1) * s_old + kv_scr[c, h]

    fill_late(1)
    for c in range(n_chunks):
        for h in range(HEADS):
            lhs = jnp.concatenate([qs_scr[crow(c), ksl(h)], sc_scr[c, h]], axis=1)
            rhs = jnp.concatenate([sb_scr[c, h], v_s[crow(c), vsl(h)]], axis=0)
            o = _dot(lhs, rhs)
            ms = jnp.mean(o * o, axis=-1, keepdims=True)
            og = o * lax.rsqrt(ms + EPS) * gn_ref[:, vsl(h)] * r_s[crow(c), vsl(h)].astype(F32)
            o_ref[crow(c), vsl(h)] = og.astype(BF16)
        if c % 2 == 1:
            fill_late(1)
    fill_late(len(late))

    @pl.when(j == pl.num_programs(1) - 1)
    def _():
        sf_ref[...] = s_scr[...]


def _mix(x2, g, ws, wup, bg, gn, s0, *, nb, rows_per_b, tm, n_pad):
    nc = tm // CHUNK
    nt = rows_per_b // tm
    rows = nb * rows_per_b
    row_spec = lambda w: pl.BlockSpec((tm, w), lambda b, i: (b * nt + i, 0))
    state_spec = pl.BlockSpec((None, HEADS, DK, DV), lambda b, i: (b, 0, 0, 0))
    s0_spec = pl.BlockSpec((None, HEADS, DK, DV), lambda b, i: (0, 0, 0, 0))
    assert s0.shape == (1, HEADS, DK, DV)
    in_specs = [row_spec(D_MODEL), _const_spec(g.shape)] + [_const_spec(w.shape) for w in ws]
    in_specs += [_const_spec(wup.shape), _const_spec(bg.shape), _const_spec(gn.shape), s0_spec]
    wide = jax.ShapeDtypeStruct((rows, D_MODEL), BF16)
    return pl.pallas_call(
        functools.partial(_mix_kernel, n_chunks=nc, n_pad=n_pad),
        grid=(nb, nt),
        in_specs=in_specs,
        out_specs=[row_spec(VAL), row_spec(WIDTH), row_spec(D_MODEL), row_spec(D_MODEL), state_spec],
        out_shape=[wide, wide, wide, wide, jax.ShapeDtypeStruct((nb, HEADS, DK, DV), F32)],
        scratch_shapes=[pltpu.VMEM((tm, KEY), BF16), pltpu.VMEM((tm, KEY), BF16),
                        pltpu.VMEM((tm, VAL), BF16), pltpu.VMEM((tm, VAL), BF16),
                        pltpu.VMEM((tm, KEY), F32),
                        pltpu.VMEM((HEADS, DK, DV), F32),
                        pltpu.VMEM((tm, KEY), F32),
                        pltpu.VMEM((tm, KEY), BF16), pltpu.VMEM((tm, KEY), BF16),
                        pltpu.VMEM((tm, KEY), BF16), pltpu.VMEM((tm, KEY), BF16),
                        pltpu.VMEM((nc, SUB, KEY), F32),
                        pltpu.VMEM((nc, HEADS, DK, DV), F32),
                        pltpu.VMEM((nc, HEADS, CHUNK, CHUNK), BF16),
                        pltpu.VMEM((nc, HEADS, DK, DV), BF16)],
        name="mix",
        compiler_params=pltpu.CompilerParams(
            dimension_semantics=("parallel", "arbitrary"), vmem_limit_bytes=VMEM_LIMIT),
    )(x2, g, *ws, wup, bg, gn, s0)


def _lambda_bar(ar, ai, ls):
    dt = jnp.exp(ls)
    mag = jnp.exp(ar * dt)
    return mag * jnp.cos(ai * dt), mag * jnp.sin(ai * dt)


def _s5disc_kernel(ars_ref, ais_ref, lss_ref, ar_ref, ai_ref, ls_ref, br_ref, bi_ref, cr_ref, ci_ref,
                   lre_o, lim_o, bblk_o, cblk_o):
    lre_s, lim_s = _lambda_bar(ars_ref[...], ais_ref[...], lss_ref[...])
    lre_o[...] = lre_s
    lim_o[...] = lim_s

    ar = ar_ref[...]
    ai = ai_ref[...]
    lre, lim = _lambda_bar(ar, ai, ls_ref[...])
    zr = lre - 1.0
    zi = lim
    den = ar * ar + ai * ai
    fr = (zr * ar + zi * ai) / den
    fi = (zi * ar - zr * ai) / den
    br = br_ref[...]
    bi = bi_ref[...]
    bb = (fr * br - fi * bi, fr * bi + fi * br)

    grp_of_row = lax.broadcasted_iota(jnp.int32, (SLAB, LANE), 0) // GSIZE
    half_of_lane = lax.broadcasted_iota(jnp.int32, (SLAB, LANE), 1) // NSTATE
    grp_of_col = lax.broadcasted_iota(jnp.int32, (LANE, SLAB), 1) // GSIZE
    half_of_row = lax.broadcasted_iota(jnp.int32, (LANE, SLAB), 0) // NSTATE
    for q in range(N_SLAB):
        ch = slice(q * SLAB, (q + 1) * SLAB)
        b_two = [jnp.concatenate([p[ch, :], p[ch, :]], axis=1) for p in bb]
        c_two = [jnp.concatenate([p[:, ch], p[:, ch]], axis=0)
                 for p in (cr_ref, ci_ref)]
        c_two[1] = -c_two[1]
        for j in range(N_TILE):
            own_b = grp_of_row == 2 * j + half_of_lane
            own_c = grp_of_col == 2 * j + half_of_row
            for p in range(2):
                cols = slice(j * TILE2 + p * LANE, j * TILE2 + (p + 1) * LANE)
                bblk_o[q, :, cols] = jnp.where(own_b, b_two[p], 0.0).astype(BF16)
                cblk_o[q, cols, :] = jnp.where(own_c, c_two[p], 0.0).astype(BF16)


def _s5disc(ars, ais, lss, ar, ai, ls, br, bi, cr, ci):
    lam = jax.ShapeDtypeStruct((N_SLAB, SLAB_STATE), F32)
    return pl.pallas_call(
        _s5disc_kernel,
        out_shape=[lam, lam, jax.ShapeDtypeStruct((N_SLAB, SLAB, 2 * SLAB_STATE), BF16),
                   jax.ShapeDtypeStruct((N_SLAB, 2 * SLAB_STATE, SLAB), BF16)],
        name="s5disc",
    )(ars, ais, lss, ar, ai, ls, br, bi, cr, ci)


S5_T = 64
S5_BLOCKS = 2
S5_PAD = 4
SCAN_TILES = 4
SCAN_GROUP = 8


def _s5_kernel(u_ref, bblk_ref, cblk_ref, lre_ref, lim_ref, d_ref, h0_ref, *rest, T, n_blk, n_cast):
    cast_in, rest = rest[:n_cast], rest[n_cast:]
    y_ref, hf_ref = rest[:2]
    cast_out, (bu_scr, s_scr, acc_scr, stg_out, st_scr) = rest[2:2 + n_cast], rest[2 + n_cast:]
    i = pl.program_id(0)
    for w_f32, w_bf16 in zip(cast_in, cast_out):
        w_bf16[...] = w_f32[...].astype(BF16)

    @pl.when(i == 0)
    def _():
        st_scr[...] = h0_ref[...]

    P = T + S5_PAD
    lane_tile = lambda lt: slice(lt * LANE, (lt + 1) * LANE)

    items = [(blk, q) for blk in range(n_blk) for q in range(N_SLAB)]
    trows = lambda blk: slice(blk * T, (blk + 1) * T)

    def in_piece(k, j):
        blk, q = items[k]
        cols = slice(j * TILE2, (j + 1) * TILE2)
        res = _dot(u_ref[:, trows(blk), q * SLAB:(q + 1) * SLAB].reshape(SUB * T, SLAB),
                   bblk_ref[q, :, cols])
        for part in range(2):
            for b in range(SUB):
                bu_scr[k % 2, 2 * j + part, b * P:b * P + T, :] = (
                    res[b * T:(b + 1) * T, part * LANE:(part + 1) * LANE])

    def out_piece(k, j):
        cols = slice(j * TILE2, (j + 1) * TILE2)
        part = _dot(s_scr[k % 2, :, cols], cblk_ref[items[k][1], cols, :])
        if j == 0:
            acc_scr[k % 2] = part
        else:
            acc_scr[k % 2] += part

    def scan_steps(k, tiles, lr, li, state, t0):
        for t in range(t0, t0 + SCAN_GROUP, 2):
            for j in tiles:
                re_c = slice(j * TILE2, j * TILE2 + LANE)
                im_c = slice(j * TILE2 + LANE, (j + 1) * TILE2)
                s_re, s_im = state[j]
                o_re, o_im = [], []
                for dt in range(2):
                    rows = pl.ds(t + dt, SUB, stride=P)
                    n_re = lr[j] * s_re - li[j] * s_im + bu_scr[k % 2, 2 * j, rows, :]
                    n_im = lr[j] * s_im + li[j] * s_re + bu_scr[k % 2, 2 * j + 1, rows, :]
                    s_re, s_im = n_re, n_im
                    o_re.append(n_re)
                    o_im.append(n_im)
                state[j] = (s_re, s_im)
                rows2 = slice(SUB * t, SUB * (t + 2))
                s_scr[k % 2, rows2, re_c] = jnp.concatenate(o_re, axis=0).astype(BF16)
                s_scr[k % 2, rows2, im_c] = jnp.concatenate(o_im, axis=0).astype(BF16)

    def regroup_out(k):
        blk, q = items[k]
        for l2 in range(SLAB // LANE):
            lt = q * (SLAB // LANE) + l2
            for t in range(T):
                stg_out[lt, pl.ds(t, SUB, stride=P), :] = (
                    acc_scr[k % 2, SUB * t:SUB * (t + 1), lane_tile(l2)])
            for b in range(SUB):
                y = (stg_out[lt, b * P:b * P + T, :]
                     + d_ref[:, lane_tile(lt)] * u_ref[b, trows(blk), lane_tile(lt)].astype(F32))
                y_ref[b, trows(blk), lane_tile(lt)] = y.astype(BF16)

    groups_per_run = T // SCAN_GROUP
    n_groups = groups_per_run * (N_TILE // SCAN_TILES)
    for stage in range(len(items) + 2):
        k_in, k_scan, k_out = stage, stage - 1, stage - 2
        pieces = []
        for j in range(N_TILE):
            if k_in < len(items):
                pieces.append(functools.partial(in_piece, k_in, j))
            if 0 <= k_out < len(items):
                pieces.append(functools.partial(out_piece, k_out, j))
        scanning = 0 <= k_scan < len(items)
        q_scan = items[k_scan][1] if scanning else None
        for g in range(n_groups):
            for p in pieces[g * len(pieces) // n_groups:(g + 1) * len(pieces) // n_groups]:
                p()
            if not scanning:
                continue
            run, g_in_run = divmod(g, groups_per_run)
            tiles = range(run * SCAN_TILES, (run + 1) * SCAN_TILES)
            if g_in_run == 0:
                lr = {j: jnp.broadcast_to(lre_ref[q_scan:q_scan + 1, lane_tile(j)], (SUB, LANE))
                      for j in tiles}
                li = {j: jnp.broadcast_to(lim_ref[q_scan:q_scan + 1, lane_tile(j)], (SUB, LANE))
                      for j in tiles}
                state = {j: (st_scr[q_scan, :, j * TILE2:j * TILE2 + LANE],
                             st_scr[q_scan, :, j * TILE2 + LANE:(j + 1) * TILE2]) for j in tiles}
            scan_steps(k_scan, tiles, lr, li, state, g_in_run * SCAN_GROUP)
            if g_in_run == groups_per_run - 1:
                for j in tiles:
                    st_scr[q_scan, :, j * TILE2:j * TILE2 + LANE] = state[j][0]
                    st_scr[q_scan, :, j * TILE2 + LANE:(j + 1) * TILE2] = state[j][1]
        if 0 <= k_out < len(items):
            regroup_out(k_out)

    @pl.when(i == pl.num_programs(0) - 1)
    def _():
        hf_ref[...] = st_scr[...]


def _s5(u3, bblk, cblk, lre, lim, d, h0, *, t_blk, n_blk=1, to_bf16=()):
    nb, t_total, _ = u3.shape
    assert nb == SUB and t_total % (n_blk * t_blk) == 0 and t_blk % (2 * SCAN_GROUP) == 0
    assert ((t_blk + S5_PAD) // 4) % 2 == 1
    n_steps = t_total // (n_blk * t_blk)
    blk = pl.BlockSpec((SUB, n_blk * t_blk, WIDTH), lambda i: (0, i, 0))
    rows = SUB * t_blk
    n_lt = WIDTH // LANE
    pitch = t_blk + S5_PAD
    assert all(w.shape[0] % (2 * SUB * n_steps) == 0 for w in to_bf16)
    cast_specs = [pl.BlockSpec((w.shape[0] // n_steps, w.shape[1]), lambda i: (i, 0)) for w in to_bf16]
    return pl.pallas_call(
        functools.partial(_s5_kernel, T=t_blk, n_blk=n_blk, n_cast=len(to_bf16)),
        grid=(n_steps,),
        in_specs=[blk, _const_spec(bblk.shape), _const_spec(cblk.shape),
                  _const_spec(lre.shape), _const_spec(lim.shape), _const_spec(d.shape),
                  _const_spec(h0.shape)] + cast_specs,
        out_specs=[blk, _const_spec(h0.shape)] + cast_specs,
        out_shape=[jax.ShapeDtypeStruct(u3.shape, BF16), jax.ShapeDtypeStruct(h0.shape, F32)]
        + [jax.ShapeDtypeStruct(w.shape, BF16) for w in to_bf16],
        scratch_shapes=[pltpu.VMEM((2, 2 * N_TILE, SUB * pitch, LANE), F32),
                        pltpu.VMEM((2, rows, 2 * SLAB_STATE), BF16),
                        pltpu.VMEM((2, rows, SLAB), F32),
                        pltpu.VMEM((n_lt, SUB * pitch, LANE), F32),
                        pltpu.VMEM(h0.shape, F32)],
        name="s5",
        compiler_params=pltpu.CompilerParams(
            dimension_semantics=("arbitrary",), vmem_limit_bytes=VMEM_LIMIT),
    )(u3, bblk, cblk, lre, lim, d, h0, *to_bf16)


FF_SPLIT = 4


POST_SPLIT = 2


def _post_kernel(x_ref, og_ref, ys_ref, sg_ref, ss_ref, wo, wglu, bglu, wout, gpost, gffn,
                 w1, w2, gfpost, out_ref):
    tm = x_ref.shape[0]
    sub = tm // POST_SPLIT
    rows = [slice(p * sub, (p + 1) * sub) for p in range(POST_SPLIT)]
    wcols = D_FF // FF_SPLIT
    st = [dict() for _ in rows]

    def s_gla(p, r):
        p["y_gla"] = _dot(og_ref[r, :], wo[...])

    def s_glu(p, r):
        ys = ys_ref[r, :].astype(F32)
        gs = 0.5 * ys * (1.0 + jnp.tanh(0.7978845608028654 * (ys + 0.044715 * (ys * ys * ys))))
        p["glu"] = _dot(gs.astype(BF16), wglu[...]) + bglu[...]

    def s_mix(p, r):
        glu = p.pop("glu")
        y_ssm = glu[:, :D_MODEL] * _sigmoid(glu[:, D_MODEL:])
        mixed = sg_ref[r, :].astype(F32) * p.pop("y_gla") + ss_ref[r, :].astype(F32) * y_ssm
        p["m2"] = _dot(mixed.astype(BF16), wout[...])

    def s_h1(p, r):
        p["h1"] = x_ref[r, :] + _rms(p.pop("m2"), gpost[...])
        p["hn"] = _rms(p["h1"], gffn[...]).astype(BF16)

    def s_ff(c):
        def run(p, r):
            a = jnp.maximum(_dot(p["hn"], w1[:, c * wcols:(c + 1) * wcols]), 0.0)
            part = _dot((a * a).astype(BF16), w2[c * wcols:(c + 1) * wcols, :])
            p["f"] = part if c == 0 else p["f"] + part
        return run

    def s_out(p, r):
        out_ref[r, :] = p["h1"] + _rms(p["f"], gfpost[...])

    for stage in [s_gla, s_glu, s_mix, s_h1] + [s_ff(c) for c in range(FF_SPLIT)] + [s_out]:
        for p, r in zip(st, rows):
            stage(p, r)


def _post(x2, og, ys, sg, ss, consts, *, nb, rows_per_b, tm):
    nt = rows_per_b // tm
    row_spec = lambda w: pl.BlockSpec((tm, w), lambda b, i: (b * nt + i, 0))
    in_specs = [row_spec(D_MODEL), row_spec(VAL), row_spec(WIDTH), row_spec(D_MODEL), row_spec(D_MODEL)]
    in_specs += [_const_spec(c.shape) for c in consts]
    return pl.pallas_call(
        _post_kernel, grid=(nb, nt), in_specs=in_specs, out_specs=row_spec(D_MODEL),
        out_shape=jax.ShapeDtypeStruct((nb * rows_per_b, D_MODEL), F32), name="post",
        compiler_params=pltpu.CompilerParams(
            dimension_semantics=("parallel", "parallel"), vmem_limit_bytes=VMEM_LIMIT),
    )(x2, og, ys, sg, ss, *consts)


def kernel(x, meta_tokens, g_mix_pre, w_in, w_gate_up, b_gate, gla_norm_g, w_o_gla, a_re, a_im,
           log_step, b_re, b_im, c_re, c_im, d_skip, w_glu, b_glu, w_out, g_mix_post, g_ffn_pre,
           w_ff1, w_ff2, g_ffn_post):
    bsz, seq, dm = x.shape
    assert dm == D_MODEL and bsz == SUB and seq % CHUNK == 0
    assert w_in.shape[0] == 1, "single layer"
    l = 0
    row = lambda t: t.reshape(1, -1).astype(F32)

    w = w_in[l]
    ws = (w[:, :W_HEAD].astype(BF16),
          jnp.pad(w[:, W_HEAD:W_HEAD + RANK], ((0, 0), (0, LANE - RANK))).astype(BF16),
          w[:, W_HEAD + RANK:].astype(BF16))
    wup = jnp.pad(w_gate_up[l], ((0, LANE - RANK), (0, 0))).astype(BF16)
    bg = row(b_gate[l])
    gpre = row(g_mix_pre[l])
    gn = row(gla_norm_g[l])

    rep = lambda t: jnp.repeat(t.astype(F32), GSIZE, axis=0)
    slabs = lambda t: t.astype(F32).reshape(N_SLAB, SLAB_STATE)
    rows_gh = lambda t: jnp.swapaxes(t, 1, 2).reshape(WIDTH, NSTATE).astype(F32)
    cols_gh = lambda t: jnp.transpose(t, (2, 0, 1)).reshape(NSTATE, WIDTH).astype(F32)
    ls_gn = jnp.broadcast_to(log_step[l][:, None], (GROUPS, NSTATE))
    lam_re, lam_im, bblk, cblk = _s5disc(
        slabs(a_re[l]), slabs(a_im[l]), slabs(ls_gn), rep(a_re[l]), rep(a_im[l]), rep(ls_gn),
        rows_gh(b_re[l]), rows_gh(b_im[l]), cols_gh(c_re[l]), cols_gh(c_im[l]))
    dsk = row(d_skip[l])

    meta_chunk = jnp.pad(meta_tokens.astype(F32), ((N_PAD, 0), (0, 0)))
    s_zero = jnp.zeros((1, HEADS, DK, DV), F32)
    _, mu, _, _, s_meta = _mix(meta_chunk, gpre, ws, wup, bg, gn, s_zero,
                               nb=1, rows_per_b=CHUNK, tm=CHUNK, n_pad=N_PAD)
    mu3 = jnp.broadcast_to(mu[None, N_PAD:], (SUB, N_META, WIDTH))
    h_zero = jnp.zeros((N_SLAB, SUB, 2 * SLAB_STATE), F32)
    _, h_meta = _s5(mu3, bblk, cblk, lam_re, lam_im, dsk, h_zero, t_blk=N_META)[:2]

    x2 = x.reshape(bsz * seq, dm)
    tm = 512 if seq % 512 == 0 else CHUNK
    og, u, sg, ss, _ = _mix(x2, gpre, ws, wup, bg, gn, s_meta, nb=bsz, rows_per_b=seq, tm=tm, n_pad=0)
    ys, _, wo_b, wglu_b, wout_b, w1_b, w2_b = _s5(
        u.reshape(bsz, seq, WIDTH), bblk, cblk, lam_re, lam_im, dsk, h_meta, t_blk=S5_T, n_blk=S5_BLOCKS,
        to_bf16=(w_o_gla[l], w_glu[l], w_out[l], w_ff1[l], w_ff2[l]))
    consts = (wo_b, wglu_b, row(b_glu[l]), wout_b, row(g_mix_post[l]), row(g_ffn_pre[l]), w1_b, w2_b,
              row(g_ffn_post[l]))
    out = _post(x2, og, ys.reshape(bsz * seq, WIDTH), sg, ss, consts, nb=bsz, rows_per_b=seq, tm=tm)
    return out.reshape(bsz, seq, dm)
```

```python
import functools

import jax
import jax.numpy as jnp
from jax import lax
from jax.experimental import pallas as pl
from jax.experimental.pallas import tpu as pltpu

F32 = jnp.float32
BF16 = jnp.bfloat16

D_MODEL = 1024
N_META = 16
CHUNK = 64
N_PAD = CHUNK - N_META
HEADS = 4
DK = 128
DV = 256
KEY = HEADS * DK
VAL = HEADS * DV
RANK = 16
TAU = 16.0
GROUPS = 64
GSIZE = 16
NSTATE = 64
WIDTH = GROUPS * GSIZE
D_FF = 4 * D_MODEL
EPS = 1e-6

LANE = 128
SUB = 8
SLAB = 256
SLAB_GROUPS = SLAB // GSIZE
N_SLAB = WIDTH // SLAB
SLAB_STATE = SLAB_GROUPS * NSTATE
N_TILE = SLAB_STATE // LANE
TILE2 = 2 * LANE
VMEM_LIMIT = 56 * 1024 * 1024


def _const_spec(shape):
    nd = len(shape)
    return pl.BlockSpec(shape, lambda *_: (0,) * nd, pipeline_mode=pl.Buffered(1))


def _rms(x, g):
    ms = jnp.mean(x * x, axis=-1, keepdims=True)
    return x * lax.rsqrt(ms + EPS) * g


def _sigmoid(x):
    return 0.5 * jnp.tanh(0.5 * x) + 0.5


def _dot(a, b):
    return jnp.dot(a, b, preferred_element_type=F32)


PROJ_SPLIT = 2
W_COLS = {"q": (0, 0, KEY), "k": (0, KEY, KEY), "v": (0, 2 * KEY, VAL), "r": (0, 2 * KEY + VAL, VAL),
          "a": (1, 0, LANE),
          "u": (2, 0, WIDTH), "zg": (2, WIDTH, D_MODEL), "zs": (2, WIDTH + D_MODEL, D_MODEL)}
W_HEAD = 2 * KEY + 2 * VAL


def _mix_kernel(x_ref, g_ref, w_head, w_a, w_tail, wup, bg, gn_ref, s0_ref,
                o_ref, u_o, zg_o, zs_o, sf_ref,
                q_s, k_s, v_s, r_s, la_s,
                s_scr, b_scr, qi_scr, ki_scr, qs_scr, ks_scr, dc_scr, kv_scr, sc_scr, sb_scr,
                *, n_chunks, n_pad):
    j = pl.program_id(1)

    @pl.when(j == 0)
    def _():
        s_scr[...] = s0_ref[...]

    xn = _rms(x_ref[...], g_ref[...]).astype(BF16)

    def mm(name, lo=0, hi=None):
        which, start, width = W_COLS[name]
        hi = width if hi is None else hi
        return _dot(xn, (w_head, w_a, w_tail)[which][:, start + lo:start + hi])

    def proj_q():
        q_s[...] = (mm("q") * (DK ** -0.5)).astype(BF16)

    def proj_k():
        k_s[...] = mm("k").astype(BF16)

    def proj_v():
        v_s[...] = mm("v").astype(BF16)

    def proj_r():
        r = mm("r")
        r_s[...] = (r * _sigmoid(r)).astype(BF16)

    def proj_piece(name, o, act, c):
        lo, hi = c * (D_MODEL // PROJ_SPLIT), (c + 1) * (D_MODEL // PROJ_SPLIT)
        o[:, lo:hi] = act(mm(name, lo, hi)).astype(BF16)

    late = [functools.partial(proj_piece, name, o, act, c)
            for name, o, act in (("u", u_o, lambda t: t), ("zg", zg_o, _sigmoid), ("zs", zs_o, _sigmoid))
            for c in range(PROJ_SPLIT)]

    def fill_late(n):
        for _ in range(min(n, len(late))):
            late.pop(0)()

    z = _dot(mm("a").astype(BF16), wup[...]) + bg[...]
    la_s[...] = (jnp.minimum(z, 0.0) - jnp.log(1.0 + jnp.exp(-jnp.abs(z)))) * (1.0 / TAU)
    if n_pad:
        la_s[0:n_pad, :] = jnp.zeros((n_pad, KEY), F32)

    row = lax.broadcasted_iota(jnp.int32, (CHUNK, CHUNK), 0)
    col = lax.broadcasted_iota(jnp.int32, (CHUNK, CHUNK), 1)
    causal = row >= col
    tril = jnp.where(causal, 1.0, 0.0).astype(BF16)
    tn = (((0,), (0,)), ((), ()))
    nt = (((1,), (1,)), ((), ()))
    crow = lambda c: slice(c * CHUNK, (c + 1) * CHUNK)
    ksl = lambda h: slice(h * DK, (h + 1) * DK)
    vsl = lambda h: slice(h * DV, (h + 1) * DV)

    proj_q()
    for c in range(n_chunks):
        b_scr[crow(c), :] = _dot(tril, la_s[crow(c), :].astype(BF16))

    proj_k()
    for c in range(n_chunks):
        b = b_scr[crow(c), :]
        b_mid = b[CHUNK // 2:CHUNK // 2 + 1, :]
        b_last = b[CHUNK - 1:CHUNK, :]
        q_in = q_s[crow(c), :].astype(F32) * jnp.exp(b - b_mid)
        k_in = k_s[crow(c), :].astype(F32) * jnp.exp(b_mid - b)
        qs_scr[crow(c), :] = (q_in * jnp.exp(b_mid)).astype(BF16)
        ks_scr[crow(c), :] = (k_in * jnp.exp(b_last - b_mid)).astype(BF16)
        qi_scr[crow(c), :] = q_in.astype(BF16)
        ki_scr[crow(c), :] = k_in.astype(BF16)
        dc_scr[c] = jnp.broadcast_to(jnp.exp(b_last), (SUB, KEY))

    proj_v()
    for c in range(n_chunks):
        for h in range(HEADS):
            sc = lax.dot_general(qi_scr[crow(c), ksl(h)], ki_scr[crow(c), ksl(h)], nt,
                                 preferred_element_type=F32)
            sc_scr[c, h] = jnp.where(causal, sc, 0.0).astype(BF16)

    proj_r()
    for c in range(n_chunks):
        for h in range(HEADS):
            kv_scr[c, h] = lax.dot_general(ks_scr[crow(c), ksl(h)], v_s[crow(c), vsl(h)], tn,
                                           preferred_element_type=F32)

    fill_late(1)
    for c in range(n_chunks):
        for h in range(HEADS):
            s_old = s_scr[h]
            sb_scr[c, h] = s_old.astype(BF16)
            dcb = jnp.broadcast_to(dc_scr[c, 0:1, ksl(h)], (DK, DK)).T
            s_scr[h] = jnp.concatenate([dcb, dcb], axis=1) * s_old + kv_scr[c, h]

    fill_late(1)
    for c in range(n_chunks):
        for h in range(HEADS):
            lhs = jnp.concatenate([qs_scr[crow(c), ksl(h)], sc_scr[c, h]], axis=1)
            rhs = jnp.concatenate([sb_scr[c, h], v_s[crow(c), vsl(h)]], axis=0)
            o = _dot(lhs, rhs)
            ms = jnp.mean(o * o, axis=-1, keepdims=True)
            og = o * lax.rsqrt(ms + EPS) * gn_ref[:, vsl(h)] * r_s[crow(c), vsl(h)].astype(F32)
            o_ref[crow(c), vsl(h)] = og.astype(BF16)
        if c % 2 == 1:
            fill_late(1)
    fill_late(len(late))

    @pl.when(j == pl.num_programs(1) - 1)
    def _():
        sf_ref[...] = s_scr[...]


def _mix(x2, g, ws, wup, bg, gn, s0, *, nb, rows_per_b, tm, n_pad):
    nc = tm // CHUNK
    nt = rows_per_b // tm
    rows = nb * rows_per_b
    row_spec = lambda w: pl.BlockSpec((tm, w), lambda b, i: (b * nt + i, 0))
    state_spec = pl.BlockSpec((None, HEADS, DK, DV), lambda b, i: (b, 0, 0, 0))
    s0_spec = pl.BlockSpec((None, HEADS, DK, DV), lambda b, i: (0, 0, 0, 0))
    assert s0.shape == (1, HEADS, DK, DV)
    in_specs = [row_spec(D_MODEL), _const_spec(g.shape)] + [_const_spec(w.shape) for w in ws]
    in_specs += [_const_spec(wup.shape), _const_spec(bg.shape), _const_spec(gn.shape), s0_spec]
    wide = jax.ShapeDtypeStruct((rows, D_MODEL), BF16)
    return pl.pallas_call(
        functools.partial(_mix_kernel, n_chunks=nc, n_pad=n_pad),
        grid=(nb, nt),
        in_specs=in_specs,
        out_specs=[row_spec(VAL), row_spec(WIDTH), row_spec(D_MODEL), row_spec(D_MODEL), state_spec],
        out_shape=[wide, wide, wide, wide, jax.ShapeDtypeStruct((nb, HEADS, DK, DV), F32)],
        scratch_shapes=[pltpu.VMEM((tm, KEY), BF16), pltpu.VMEM((tm, KEY), BF16),
                        pltpu.VMEM((tm, VAL), BF16), pltpu.VMEM((tm, VAL), BF16),
                        pltpu.VMEM((tm, KEY), F32),
                        pltpu.VMEM((HEADS, DK, DV), F32),
                        pltpu.VMEM((tm, KEY), F32),
                        pltpu.VMEM((tm, KEY), BF16), pltpu.VMEM((tm, KEY), BF16),
                        pltpu.VMEM((tm, KEY), BF16), pltpu.VMEM((tm, KEY), BF16),
                        pltpu.VMEM((nc, SUB, KEY), F32),
                        pltpu.VMEM((nc, HEADS, DK, DV), F32),
                        pltpu.VMEM((nc, HEADS, CHUNK, CHUNK), BF16),
                        pltpu.VMEM((nc, HEADS, DK, DV), BF16)],
        name="mix",
        compiler_params=pltpu.CompilerParams(
            dimension_semantics=("parallel", "arbitrary"), vmem_limit_bytes=VMEM_LIMIT),
    )(x2, g, *ws, wup, bg, gn, s0)


def _lambda_bar(ar, ai, ls):
    dt = jnp.exp(ls)
    mag = jnp.exp(ar * dt)
    return mag * jnp.cos(ai * dt), mag * jnp.sin(ai * dt)


def _s5disc_kernel(ars_ref, ais_ref, lss_ref, ar_ref, ai_ref, ls_ref, br_ref, bi_ref, cr_ref, ci_ref,
                   lre_o, lim_o, bblk_o, cblk_o):
    lre_s, lim_s = _lambda_bar(ars_ref[...], ais_ref[...], lss_ref[...])
    lre_o[...] = lre_s
    lim_o[...] = lim_s

    ar = ar_ref[...]
    ai = ai_ref[...]
    lre, lim = _lambda_bar(ar, ai, ls_ref[...])
    zr = lre - 1.0
    zi = lim
    den = ar * ar + ai * ai
    fr = (zr * ar + zi * ai) / den
    fi = (zi * ar - zr * ai) / den
    br = br_ref[...]
    bi = bi_ref[...]
    bb = (fr * br - fi * bi, fr * bi + fi * br)

    grp_of_row = lax.broadcasted_iota(jnp.int32, (SLAB, LANE), 0) // GSIZE
    half_of_lane = lax.broadcasted_iota(jnp.int32, (SLAB, LANE), 1) // NSTATE
    grp_of_col = lax.broadcasted_iota(jnp.int32, (LANE, SLAB), 1) // GSIZE
    half_of_row = lax.broadcasted_iota(jnp.int32, (LANE, SLAB), 0) // NSTATE
    for q in range(N_SLAB):
        ch = slice(q * SLAB, (q + 1) * SLAB)
        b_two = [jnp.concatenate([p[ch, :], p[ch, :]], axis=1) for p in bb]
        c_two = [jnp.concatenate([p[:, ch], p[:, ch]], axis=0)
                 for p in (cr_ref, ci_ref)]
        c_two[1] = -c_two[1]
        for j in range(N_TILE):
            own_b = grp_of_row == 2 * j + half_of_lane
            own_c = grp_of_col == 2 * j + half_of_row
            for p in range(2):
                cols = slice(j * TILE2 + p * LANE, j * TILE2 + (p + 1) * LANE)
                bblk_o[q, :, cols] = jnp.where(own_b, b_two[p], 0.0).astype(BF16)
                cblk_o[q, cols, :] = jnp.where(own_c, c_two[p], 0.0).astype(BF16)


def _s5disc(ars, ais, lss, ar, ai, ls, br, bi, cr, ci):
    lam = jax.ShapeDtypeStruct((N_SLAB, SLAB_STATE), F32)
    return pl.pallas_call(
        _s5disc_kernel,
        out_shape=[lam, lam, jax.ShapeDtypeStruct((N_SLAB, SLAB, 2 * SLAB_STATE), BF16),
                   jax.ShapeDtypeStruct((N_SLAB, 2 * SLAB_STATE, SLAB), BF16)],
        name="s5disc",
    )(ars, ais, lss, ar, ai, ls, br, bi, cr, ci)


S5_T = 64
S5_BLOCKS = 2
S5_PAD = 4
SCAN_TILES = 4
SCAN_GROUP = 8


def _s5_kernel(u_ref, bblk_ref, cblk_ref, lre_ref, lim_ref, d_ref, h0_ref, *rest, T, n_blk, n_cast):
    cast_in, rest = rest[:n_cast], rest[n_cast:]
    y_ref, hf_ref = rest[:2]
    cast_out, (bu_scr, s_scr, acc_scr, stg_out, st_scr) = rest[2:2 + n_cast], rest[2 + n_cast:]
    i = pl.program_id(0)
    for w_f32, w_bf16 in zip(cast_in, cast_out):
        w_bf16[...] = w_f32[...].astype(BF16)

    @pl.when(i == 0)
    def _():
        st_scr[...] = h0_ref[...]

    P = T + S5_PAD
    lane_tile = lambda lt: slice(lt * LANE, (lt + 1) * LANE)

    items = [(blk, q) for blk in range(n_blk) for q in range(N_SLAB)]
    trows = lambda blk: slice(blk * T, (blk + 1) * T)

    def in_piece(k, j):
        blk, q = items[k]
        cols = slice(j * TILE2, (j + 1) * TILE2)
        res = _dot(u_ref[:, trows(blk), q * SLAB:(q + 1) * SLAB].reshape(SUB * T, SLAB),
                   bblk_ref[q, :, cols])
        for part in range(2):
            for b in range(SUB):
                bu_scr[k % 2, 2 * j + part, b * P:b * P + T, :] = (
                    res[b * T:(b + 1) * T, part * LANE:(part + 1) * LANE])

    def out_piece(k, j):
        cols = slice(j * TILE2, (j + 1) * TILE2)
        part = _dot(s_scr[k % 2, :, cols], cblk_ref[items[k][1], cols, :])
        if j == 0:
            acc_scr[k % 2] = part
        else:
            acc_scr[k % 2] += part

    def scan_steps(k, tiles, lr, li, state, t0):
        for t in range(t0, t0 + SCAN_GROUP, 2):
            for j in tiles:
                re_c = slice(j * TILE2, j * TILE2 + LANE)
                im_c = slice(j * TILE2 + LANE, (j + 1) * TILE2)
                s_re, s_im = state[j]
                o_re, o_im = [], []
                for dt in range(2):
                    rows = pl.ds(t + dt, SUB, stride=P)
                    n_re = lr[j] * s_re - li[j] * s_im + bu_scr[k % 2, 2 * j, rows, :]
                    n_im = lr[j] * s_im + li[j] * s_re + bu_scr[k % 2, 2 * j + 1, rows, :]
                    s_re, s_im = n_re, n_im
                    o_re.append(n_re)
                    o_im.append(n_im)
                state[j] = (s_re, s_im)
                rows2 = slice(SUB * t, SUB * (t + 2))
                s_scr[k % 2, rows2, re_c] = jnp.concatenate(o_re, axis=0).astype(BF16)
                s_scr[k % 2, rows2, im_c] = jnp.concatenate(o_im, axis=0).astype(BF16)

    def regroup_out(k):
        blk, q = items[k]
        for l2 in range(SLAB // LANE):
            lt = q * (SLAB // LANE) + l2
            for t in range(T):
                stg_out[lt, pl.ds(t, SUB, stride=P), :] = (
                    acc_scr[k % 2, SUB * t:SUB * (t + 1), lane_tile(l2)])
            for b in range(SUB):
                y = (stg_out[lt, b * P:b * P + T, :]
                     + d_ref[:, lane_tile(lt)] * u_ref[b, trows(blk), lane_tile(lt)].astype(F32))
                y_ref[b, trows(blk), lane_tile(lt)] = y.astype(BF16)

    groups_per_run = T // SCAN_GROUP
    n_groups = groups_per_run * (N_TILE // SCAN_TILES)
    for stage in range(len(items) + 2):
        k_in, k_scan, k_out = stage, stage - 1, stage - 2
        pieces = []
        for j in range(N_TILE):
            if k_in < len(items):
                pieces.append(functools.partial(in_piece, k_in, j))
            if 0 <= k_out < len(items):
                pieces.append(functools.partial(out_piece, k_out, j))
        scanning = 0 <= k_scan < len(items)
        q_scan = items[k_scan][1] if scanning else None
        for g in range(n_groups):
            for p in pieces[g * len(pieces) // n_groups:(g + 1) * len(pieces) // n_groups]:
                p()
            if not scanning:
                continue
            run, g_in_run = divmod(g, groups_per_run)
            tiles = range(run * SCAN_TILES, (run + 1) * SCAN_TILES)
            if g_in_run == 0:
                lr = {j: jnp.broadcast_to(lre_ref[q_scan:q_scan + 1, lane_tile(j)], (SUB, LANE))
                      for j in tiles}
                li = {j: jnp.broadcast_to(lim_ref[q_scan:q_scan + 1, lane_tile(j)], (SUB, LANE))
                      for j in tiles}
                state = {j: (st_scr[q_scan, :, j * TILE2:j * TILE2 + LANE],
                             st_scr[q_scan, :, j * TILE2 + LANE:(j + 1) * TILE2]) for j in tiles}
            scan_steps(k_scan, tiles, lr, li, state, g_in_run * SCAN_GROUP)
            if g_in_run == groups_per_run - 1:
                for j in tiles:
                    st_scr[q_scan, :, j * TILE2:j * TILE2 + LANE] = state[j][0]
                    st_scr[q_scan, :, j * TILE2 + LANE:(j + 1) * TILE2] = state[j][1]
        if 0 <= k_out < len(items):
            regroup_out(k_out)

    @pl.when(i == pl.num_programs(0) - 1)
    def _():
        hf_ref[...] = st_scr[...]


def _s5(u3, bblk, cblk, lre, lim, d, h0, *, t_blk, n_blk=1, to_bf16=()):
    nb, t_total, _ = u3.shape
    assert nb == SUB and t_total % (n_blk * t_blk) == 0 and t_blk % (2 * SCAN_GROUP) == 0
    assert ((t_blk + S5_PAD) // 4) % 2 == 1
    n_steps = t_total // (n_blk * t_blk)
    blk = pl.BlockSpec((SUB, n_blk * t_blk, WIDTH), lambda i: (0, i, 0))
    rows = SUB * t_blk
    n_lt = WIDTH // LANE
    pitch = t_blk + S5_PAD
    assert all(w.shape[0] % (2 * SUB * n_steps) == 0 for w in to_bf16)
    cast_specs = [pl.BlockSpec((w.shape[0] // n_steps, w.shape[1]), lambda i: (i, 0)) for w in to_bf16]
    return pl.pallas_call(
        functools.partial(_s5_kernel, T=t_blk, n_blk=n_blk, n_cast=len(to_bf16)),
        grid=(n_steps,),
        in_specs=[blk, _const_spec(bblk.shape), _const_spec(cblk.shape),
                  _const_spec(lre.shape), _const_spec(lim.shape), _const_spec(d.shape),
                  _const_spec(h0.shape)] + cast_specs,
        out_specs=[blk, _const_spec(h0.shape)] + cast_specs,
        out_shape=[jax.ShapeDtypeStruct(u3.shape, BF16), jax.ShapeDtypeStruct(h0.shape, F32)]
        + [jax.ShapeDtypeStruct(w.shape, BF16) for w in to_bf16],
        scratch_shapes=[pltpu.VMEM((2, 2 * N_TILE, SUB * pitch, LANE), F32),
                        pltpu.VMEM((2, rows, 2 * SLAB_STATE), BF16),
                        pltpu.VMEM((2, rows, SLAB), F32),
                        pltpu.VMEM((n_lt, SUB * pitch, LANE), F32),
                        pltpu.VMEM(h0.shape, F32)],
        name="s5",
        compiler_params=pltpu.CompilerParams(
            dimension_semantics=("arbitrary",), vmem_limit_bytes=VMEM_LIMIT),
    )(u3, bblk, cblk, lre, lim, d, h0, *to_bf16)


FF_SPLIT = 4


POST_SPLIT = 2


def _post_kernel(x_ref, og_ref, ys_ref, sg_ref, ss_ref, wo, wglu, bglu, wout, gpost, gffn,
                 w1, w2, gfpost, out_ref):
    tm = x_ref.shape[0]
    sub = tm // POST_SPLIT
    rows = [slice(p * sub, (p + 1) * sub) for p in range(POST_SPLIT)]
    wcols = D_FF // FF_SPLIT
    st = [dict() for _ in rows]

    def s_gla(p, r):
        p["y_gla"] = _dot(og_ref[r, :], wo[...])

    def s_glu(p, r):
        ys = ys_ref[r, :].astype(F32)
        gs = 0.5 * ys * (1.0 + jnp.tanh(0.7978845608028654 * (ys + 0.044715 * (ys * ys * ys))))
        p["glu"] = _dot(gs.astype(BF16), wglu[...]) + bglu[...]

    def s_mix(p, r):
        glu = p.pop("glu")
        y_ssm = glu[:, :D_MODEL] * _sigmoid(glu[:, D_MODEL:])
        mixed = sg_ref[r, :].astype(F32) * p.pop("y_gla") + ss_ref[r, :].astype(F32) * y_ssm
        p["m2"] = _dot(mixed.astype(BF16), wout[...])

    def s_h1(p, r):
        p["h1"] = x_ref[r, :] + _rms(p.pop("m2"), gpost[...])
        p["hn"] = _rms(p["h1"], gffn[...]).astype(BF16)

    def s_ff(c):
        def run(p, r):
            a = jnp.maximum(_dot(p["hn"], w1[:, c * wcols:(c + 1) * wcols]), 0.0)
            part = _dot((a * a).astype(BF16), w2[c * wcols:(c + 1) * wcols, :])
            p["f"] = part if c == 0 else p["f"] + part
        return run

    def s_out(p, r):
        out_ref[r, :] = p["h1"] + _rms(p["f"], gfpost[...])

    for stage in [s_gla, s_glu, s_mix, s_h1] + [s_ff(c) for c in range(FF_SPLIT)] + [s_out]:
        for p, r in zip(st, rows):
            stage(p, r)


def _post(x2, og, ys, sg, ss, consts, *, nb, rows_per_b, tm):
    nt = rows_per_b // tm
    row_spec = lambda w: pl.BlockSpec((tm, w), lambda b, i: (b * nt + i, 0))
    in_specs = [row_spec(D_MODEL), row_spec(VAL), row_spec(WIDTH), row_spec(D_MODEL), row_spec(D_MODEL)]
    in_specs += [_const_spec(c.shape) for c in consts]
    return pl.pallas_call(
        _post_kernel, grid=(nb, nt), in_specs=in_specs, out_specs=row_spec(D_MODEL),
        out_shape=jax.ShapeDtypeStruct((nb * rows_per_b, D_MODEL), F32), name="post",
        compiler_params=pltpu.CompilerParams(
            dimension_semantics=("parallel", "parallel"), vmem_limit_bytes=VMEM_LIMIT),
    )(x2, og, ys, sg, ss, *consts)


def kernel(x, meta_tokens, g_mix_pre, w_in, w_gate_up, b_gate, gla_norm_g, w_o_gla, a_re, a_im,
           log_step, b_re, b_im, c_re, c_im, d_skip, w_glu, b_glu, w_out, g_mix_post, g_ffn_pre,
           w_ff1, w_ff2, g_ffn_post):
    bsz, seq, dm = x.shape
    assert dm == D_MODEL and bsz == SUB and seq % CHUNK == 0
    assert w_in.shape[0] == 1, "single layer"
    l = 0
    row = lambda t: t.reshape(1, -1).astype(F32)

    w = w_in[l]
    ws = (w[:, :W_HEAD].astype(BF16),
          jnp.pad(w[:, W_HEAD:W_HEAD + RANK], ((0, 0), (0, LANE - RANK))).astype(BF16),
          w[:, W_HEAD + RANK:].astype(BF16))
    wup = jnp.pad(w_gate_up[l], ((0, LANE - RANK), (0, 0))).astype(BF16)
    bg = row(b_gate[l])
    gpre = row(g_mix_pre[l])
    gn = row(gla_norm_g[l])

    rep = lambda t: jnp.repeat(t.astype(F32), GSIZE, axis=0)
    slabs = lambda t: t.astype(F32).reshape(N_SLAB, SLAB_STATE)
    rows_gh = lambda t: jnp.swapaxes(t, 1, 2).reshape(WIDTH, NSTATE).astype(F32)
    cols_gh = lambda t: jnp.transpose(t, (2, 0, 1)).reshape(NSTATE, WIDTH).astype(F32)
    ls_gn = jnp.broadcast_to(log_step[l][:, None], (GROUPS, NSTATE))
    lam_re, lam_im, bblk, cblk = _s5disc(
        slabs(a_re[l]), slabs(a_im[l]), slabs(ls_gn), rep(a_re[l]), rep(a_im[l]), rep(ls_gn),
        rows_gh(b_re[l]), rows_gh(b_im[l]), cols_gh(c_re[l]), cols_gh(c_im[l]))
    dsk = row(d_skip[l])

    meta_chunk = jnp.pad(meta_tokens.astype(F32), ((N_PAD, 0), (0, 0)))
    s_zero = jnp.zeros((1, HEADS, DK, DV), F32)
    _, mu, _, _, s_meta = _mix(meta_chunk, gpre, ws, wup, bg, gn, s_zero,
                               nb=1, rows_per_b=CHUNK, tm=CHUNK, n_pad=N_PAD)
    mu3 = jnp.broadcast_to(mu[None, N_PAD:], (SUB, N_META, WIDTH))
    h_zero = jnp.zeros((N_SLAB, SUB, 2 * SLAB_STATE), F32)
    _, h_meta = _s5(mu3, bblk, cblk, lam_re, lam_im, dsk, h_zero, t_blk=N_META)[:2]

    x2 = x.reshape(bsz * seq, dm)
    tm = 512 if seq % 512 == 0 else CHUNK
    og, u, sg, ss, _ = _mix(x2, gpre, ws, wup, bg, gn, s_meta, nb=bsz, rows_per_b=seq, tm=tm, n_pad=0)
    ys, _, wo_b, wglu_b, wout_b, w1_b, w2_b = _s5(
        u.reshape(bsz, seq, WIDTH), bblk, cblk, lam_re, lam_im, dsk, h_meta, t_blk=S5_T, n_blk=S5_BLOCKS,
        to_bf16=(w_o_gla[l], w_glu[l], w_out[l], w_ff1[l], w_ff2[l]))
    consts = (wo_b, wglu_b, row(b_glu[l]), wout_b, row(g_mix_post[l]), row(g_ffn_pre[l]), w1_b, w2_b,
              row(g_ffn_post[l]))
    out = _post(x2, og, ys.reshape(bsz * seq, WIDTH), sg, ss, consts, nb=bsz, rows_per_b=seq, tm=tm)
    return out.reshape(bsz, seq, dm)
```
